```python
import math
import functools
import jax
import jax.numpy as jnp
from jax import lax
import numpy as np

D_MODEL = 1024
BATCH = 8
SEQ = 2048
DEPTH = 4
DEC_BATCH = 32
DEC_SEQ = 4
PAST_LEN = 8192
PAGE_SIZE = 128

D_FF = 2816
C_CONV = 512
CONV_WIDTH = 31
N_HEADS = 8
N_KV_HEADS = 2
HEAD_DIM = 64
GROUP_SIZE = N_HEADS // N_KV_HEADS
N_IDX_HEADS = 8
IDX_DIM = 64
TOPK_MAX = 256
Q_BLOCK = 128
N_BUCKETS = 32
MAX_DISTANCE = 128
C_GMLP = 512
N_GMLP_GROUPS = 4
GMLP_GROUP_DIM = C_GMLP // N_GMLP_GROUPS
GMLP_CHUNK = 128
N_BRANCHES = 3
EPS = 1e-6
SPLITS = (2 * C_CONV, N_HEADS * HEAD_DIM, N_KV_HEADS * HEAD_DIM, N_KV_HEADS * HEAD_DIM, N_IDX_HEADS * IDX_DIM, IDX_DIM, N_IDX_HEADS, 2 * C_GMLP, N_BRANCHES * D_MODEL)
P_IN = 2 * C_CONV + (N_HEADS + 2 * N_KV_HEADS) * HEAD_DIM + (N_IDX_HEADS + 1) * IDX_DIM + N_IDX_HEADS + 2 * C_GMLP + N_BRANCHES * D_MODEL

kernel_name = 'hybrid_conv_dsa_gmlp_decode_step'


def rms_norm(x, g):
    xf = x.astype(jnp.float32)
    y = xf * lax.rsqrt(jnp.mean(xf * xf, axis=-1, keepdims=True) + EPS)
    return (y * g.astype(jnp.float32)).astype(x.dtype)


def layer_norm(x, g, b):
    xf = x.astype(jnp.float32)
    mu = jnp.mean(xf, axis=-1, keepdims=True)
    xc = xf - mu
    y = xc * lax.rsqrt(jnp.mean(xc * xc, axis=-1, keepdims=True) + EPS)
    return (y * g.astype(jnp.float32) + b.astype(jnp.float32)).astype(x.dtype)


def swiglu(x, w_gate, w_up, w_down):
    return (jax.nn.silu(x @ w_gate) * (x @ w_up)) @ w_down


def split_cols(z):
    out = []
    o = 0
    for w in SPLITS:
        out.append(z[..., o:o + w])
        o += w
    return out


def rel_bucket(dist):
    max_exact = N_BUCKETS // 2
    d = jnp.maximum(dist, 1).astype(jnp.float32)
    large = max_exact + (jnp.log(d / max_exact) / math.log(MAX_DISTANCE / max_exact) * (N_BUCKETS - max_exact)).astype(jnp.int32)
    large = jnp.minimum(large, N_BUCKETS - 1)
    return jnp.where(dist < max_exact, dist, large)


def conv_module(glu, prefix, conv_w, conv_b, ln_g, ln_b, w_o):
    xin = jnp.concatenate([prefix, glu], axis=1)
    y = lax.conv_general_dilated(xin, conv_w[:, None, :], window_strides=(1,), padding='VALID',
                                 dimension_numbers=('NWC', 'WIO', 'NWC'), feature_group_count=C_CONV)
    y = jax.nn.silu(layer_norm(y + conv_b, ln_g, ln_b))
    return y @ w_o, xin[:, -(CONV_WIDTH - 1):]


def gmlp_branch(uv, ln_g, ln_b, ws, bs, w_o):
    uv = jax.nn.gelu(uv)
    u = uv[..., :C_GMLP]
    v = layer_norm(uv[..., C_GMLP:], ln_g, ln_b)
    B, T, _ = u.shape
    tc = min(T, GMLP_CHUNK)
    nc = T // tc
    causal = jnp.tril(jnp.ones((tc, tc), dtype=bool))
    wsm = jnp.where(causal[None], ws[:, :tc, :tc], 0)
    vc = v.reshape(B, nc, tc, N_GMLP_GROUPS, GMLP_GROUP_DIM)
    mixed = jnp.einsum('gts,bnsgc->bntgc', wsm, vc) + bs[:, :tc].T[None, None, :, :, None]
    y = u * mixed.reshape(B, T, C_GMLP)
    return y @ w_o, v


def dsa_attend(q, qi, wi, q_pos, ki_all, gather_kv, rel_bias, k_sel):
    f32 = jnp.float32
    L = ki_all.shape[1]
    s = jnp.einsum('bqhd,bld->bqhl', qi.astype(f32), ki_all.astype(f32))
    score = jnp.einsum('bqhl,bqh->bql', jax.nn.relu(s), wi.astype(f32))
    admissible = jnp.arange(L, dtype=jnp.int32)[None, :] <= q_pos[:, None]
    score = jnp.where(admissible[None], score, -jnp.inf)
    _, sel = lax.top_k(score, k_sel)
    valid = sel <= q_pos[None, :, None]
    kg, vg = gather_kv(sel)
    B, Q = q.shape[:2]
    qg = q.reshape(B, Q, N_KV_HEADS, GROUP_SIZE, HEAD_DIM)
    logits = jnp.einsum('bqhgd,bqkhd->bqhgk', qg.astype(f32), kg.astype(f32)) * (HEAD_DIM ** -0.5)
    bucket = rel_bucket(jnp.maximum(q_pos[None, :, None] - sel, 0))
    bias = rel_bias[bucket].astype(f32).reshape(B, Q, k_sel, N_KV_HEADS, GROUP_SIZE).transpose(0, 1, 3, 4, 2)
    logits = jnp.where(valid[:, :, None, None, :], logits + bias, -jnp.inf)
    p = jax.nn.softmax(logits, axis=-1)
    o = jnp.einsum('bqhgk,bqkhd->bqhgd', p, vg.astype(f32))
    return o.reshape(B, Q, N_HEADS * HEAD_DIM).astype(q.dtype)


def attn_prompt(q, k, v, qi, ki, wi, rel_bias):
    B, S = q.shape[:2]
    k_sel = min(TOPK_MAX, S // 4)
    nb = S // Q_BLOCK
    bidx = jnp.arange(B)[:, None, None]

    def gather(sel):
        return k[bidx, sel], v[bidx, sel]

    def blockify(a):
        return a.reshape(B, nb, Q_BLOCK, *a.shape[2:]).swapaxes(0, 1)

    pos = jnp.arange(S, dtype=jnp.int32).reshape(nb, Q_BLOCK)

    def one_block(args):
        qb, qib, wib, pb = args
        return dsa_attend(qb, qib, wib, pb, ki, gather, rel_bias, k_sel)

    o = lax.map(one_block, (blockify(q), blockify(qi), blockify(wi), pos))
    return o.swapaxes(0, 1).reshape(B, S, N_HEADS * HEAD_DIM)


def attn_sample(q, k, v, qi, ki, wi, layer, cache_k, cache_v, cache_idx_k, page_table, rel_bias):
    DB, T = q.shape[:2]
    page = cache_k.shape[2]
    past = page_table.shape[1] * page
    k_sel = min(TOPK_MAX, (past + T) // 4)
    ki_past = cache_idx_k[layer, page_table].reshape(DB, past, IDX_DIM)
    ki_all = jnp.concatenate([ki_past, ki.astype(ki_past.dtype)], axis=1)
    bidx = jnp.arange(DB)[:, None, None]

    def gather(sel):
        in_past = (sel < past)[..., None, None]
        sp = jnp.minimum(sel, past - 1)
        phys = page_table[bidx, sp // page]
        off = sp % page
        sn = jnp.clip(sel - past, 0, T - 1)
        kg = jnp.where(in_past, cache_k[layer, phys, off], k[bidx, sn])
        vg = jnp.where(in_past, cache_v[layer, phys, off], v[bidx, sn])
        return kg, vg

    pos = past + jnp.arange(T, dtype=jnp.int32)
    return dsa_attend(q, qi, wi, pos, ki_all, gather, rel_bias, k_sel)


def token_mix(h, attn_fn, conv_prefix, w_in, conv_w, conv_b, conv_ln_g, conv_ln_b, w_conv_out,
              w_attn_out, gmlp_ln_g, gmlp_ln_b, gmlp_ws, gmlp_bs, w_gmlp_out, w_out):
    B, T = h.shape[:2]
    a, qa, ka, va, qia, kia, wia, uv, g = split_cols(h @ w_in)
    glu = a[..., :C_CONV] * jax.nn.sigmoid(a[..., C_CONV:])
    y_a, conv_state = conv_module(glu, conv_prefix, conv_w, conv_b, conv_ln_g, conv_ln_b, w_conv_out)
    q = qa.reshape(B, T, N_HEADS, HEAD_DIM)
    k = ka.reshape(B, T, N_KV_HEADS, HEAD_DIM)
    v = va.reshape(B, T, N_KV_HEADS, HEAD_DIM)
    qi = qia.reshape(B, T, N_IDX_HEADS, IDX_DIM)
    y_b = attn_fn(q, k, v, qi, kia, wia) @ w_attn_out
    y_c, v_rows = gmlp_branch(uv, gmlp_ln_g, gmlp_ln_b, gmlp_ws, gmlp_bs, w_gmlp_out)
    gate = jax.nn.sigmoid(g.reshape(B, T, N_BRANCHES, D_MODEL))
    m = gate[..., 0, :] * y_a + gate[..., 1, :] * y_b + gate[..., 2, :] * y_c
    return m @ w_out, (k, v, kia, conv_state, v_rows)


def trunk_layer(x, attn_fn, conv_prefix, f1_pre, f1_post, f1_wg, f1_wu, f1_wd, m_pre, m_post,
                w_in, conv_w, conv_b, conv_ln_g, conv_ln_b, w_conv_out, w_attn_out,
                gmlp_ln_g, gmlp_ln_b, gmlp_ws, gmlp_bs, w_gmlp_out, w_out,
                f2_pre, f2_post, f2_wg, f2_wu, f2_wd):
    x = x + 0.5 * rms_norm(swiglu(rms_norm(x, f1_pre), f1_wg, f1_wu, f1_wd), f1_post)
    mix, st = token_mix(rms_norm(x, m_pre), attn_fn, conv_prefix, w_in, conv_w, conv_b, conv_ln_g, conv_ln_b,
                        w_conv_out, w_attn_out, gmlp_ln_g, gmlp_ln_b, gmlp_ws, gmlp_bs, w_gmlp_out, w_out)
    x = x + rms_norm(mix, m_post)
    x = x + 0.5 * rms_norm(swiglu(rms_norm(x, f2_pre), f2_wg, f2_wu, f2_wd), f2_post)
    return x, st


def setup_inputs(seed: int = 0):
    key = jax.random.key(seed)
    ks = jax.random.split(key, 34)
    n_pages = PAST_LEN // PAGE_SIZE
    n_used = DEC_BATCH * n_pages
    n_pool = n_used + max(1, n_used // 4)
    f32 = jnp.float32

    def nrm(k, shape, scale):
        return jax.random.normal(k, shape, f32) * scale

    def gain(k, shape):
        return 1.0 + nrm(k, shape, 0.01)

    page_table = jax.random.permutation(ks[0], n_pool)[:n_used].reshape(DEC_BATCH, n_pages).astype(jnp.int32)
    return {
        'x_prompt': nrm(ks[1], (BATCH, SEQ, D_MODEL), 1.0),
        'x_sample': nrm(ks[2], (DEC_BATCH, DEC_SEQ, D_MODEL), 1.0),
        'cache_k': nrm(ks[3], (DEPTH, n_pool, PAGE_SIZE, N_KV_HEADS, HEAD_DIM), 1.0),
        'cache_v': nrm(ks[4], (DEPTH, n_pool, PAGE_SIZE, N_KV_HEADS, HEAD_DIM), 1.0),
        'cache_idx_k': nrm(ks[5], (DEPTH, n_pool, PAGE_SIZE, IDX_DIM), 1.0),
        'state_conv': nrm(ks[6], (DEPTH, DEC_BATCH, CONV_WIDTH - 1, C_CONV), 0.5),
        'page_table': page_table,
        'ffn1_norm_pre': gain(ks[7], (DEPTH, D_MODEL)),
        'ffn1_norm_post': gain(ks[8], (DEPTH, D_MODEL)),
        'ffn1_w_gate': nrm(ks[9], (DEPTH, D_MODEL, D_FF), D_MODEL ** -0.5),
        'ffn1_w_up': nrm(ks[10], (DEPTH, D_MODEL, D_FF), D_MODEL ** -0.5),
        'ffn1_w_down': nrm(ks[11], (DEPTH, D_FF, D_MODEL), D_FF ** -0.5),
        'mix_norm_pre': gain(ks[12], (DEPTH, D_MODEL)),
        'mix_norm_post': gain(ks[13], (DEPTH, D_MODEL)),
        'w_in': nrm(ks[14], (DEPTH, D_MODEL, P_IN), D_MODEL ** -0.5),
        'conv_w': nrm(ks[15], (DEPTH, CONV_WIDTH, C_CONV), CONV_WIDTH ** -0.5),
        'conv_b': nrm(ks[16], (DEPTH, C_CONV), 0.01),
        'conv_ln_g': gain(ks[17], (DEPTH, C_CONV)),
        'conv_ln_b': nrm(ks[18], (DEPTH, C_CONV), 0.01),
        'w_conv_out': nrm(ks[19], (DEPTH, C_CONV, D_MODEL), C_CONV ** -0.5),
        'w_attn_out': nrm(ks[20], (DEPTH, N_HEADS * HEAD_DIM, D_MODEL), (N_HEADS * HEAD_DIM) ** -0.5),
        'rel_bias': nrm(ks[21], (N_BUCKETS, N_HEADS), 0.5),
        'gmlp_ln_g': gain(ks[22], (DEPTH, C_GMLP)),
        'gmlp_ln_b': nrm(ks[23], (DEPTH, C_GMLP), 0.01),
        'gmlp_ws': nrm(ks[24], (DEPTH, N_GMLP_GROUPS, GMLP_CHUNK, GMLP_CHUNK), GMLP_CHUNK ** -0.5),
        'gmlp_bs': gain(ks[25], (DEPTH, N_GMLP_GROUPS, GMLP_CHUNK)),
        'w_gmlp_out': nrm(ks[26], (DEPTH, C_GMLP, D_MODEL), C_GMLP ** -0.5),
        'w_out': nrm(ks[27], (DEPTH, D_MODEL, D_MODEL), D_MODEL ** -0.5),
        'ffn2_norm_pre': gain(ks[28], (DEPTH, D_MODEL)),
        'ffn2_norm_post': gain(ks[29], (DEPTH, D_MODEL)),
        'ffn2_w_gate': nrm(ks[30], (DEPTH, D_MODEL, D_FF), D_MODEL ** -0.5),
        'ffn2_w_up': nrm(ks[31], (DEPTH, D_MODEL, D_FF), D_MODEL ** -0.5),
        'ffn2_w_down': nrm(ks[32], (DEPTH, D_FF, D_MODEL), D_FF ** -0.5),
    }


def reference(x_prompt, x_sample, cache_k, cache_v, cache_idx_k, state_conv, page_table,
              ffn1_norm_pre, ffn1_norm_post, ffn1_w_gate, ffn1_w_up, ffn1_w_down,
              mix_norm_pre, mix_norm_post, w_in, conv_w, conv_b, conv_ln_g, conv_ln_b,
              w_conv_out, w_attn_out, rel_bias, gmlp_ln_g, gmlp_ln_b, gmlp_ws, gmlp_bs,
              w_gmlp_out, w_out, ffn2_norm_pre, ffn2_norm_post, ffn2_w_gate, ffn2_w_up, ffn2_w_down):
    layer_weights = (ffn1_norm_pre, ffn1_norm_post, ffn1_w_gate, ffn1_w_up, ffn1_w_down,
                     mix_norm_pre, mix_norm_post, w_in, conv_w, conv_b, conv_ln_g, conv_ln_b,
                     w_conv_out, w_attn_out, gmlp_ln_g, gmlp_ln_b, gmlp_ws, gmlp_bs, w_gmlp_out, w_out,
                     ffn2_norm_pre, ffn2_norm_post, ffn2_w_gate, ffn2_w_up, ffn2_w_down)
    attn_p = functools.partial(attn_prompt, rel_bias=rel_bias)
    yp = x_prompt
    ys = x_sample
    kp, vp, kip, cp = [], [], [], []
    kss, vss, kis, css, gvs = [], [], [], [], []
    for l in range(DEPTH):
        lw = [w[l] for w in layer_weights]
        prefix_p = jnp.zeros((yp.shape[0], CONV_WIDTH - 1, C_CONV), yp.dtype)
        yp, (k_, v_, ki_, c_, _) = trunk_layer(yp, attn_p, prefix_p, *lw)
        kp.append(k_)
        vp.append(v_)
        kip.append(ki_)
        cp.append(c_)
        attn_s = functools.partial(attn_sample, layer=l, cache_k=cache_k, cache_v=cache_v,
                                   cache_idx_k=cache_idx_k, page_table=page_table, rel_bias=rel_bias)
        ys, (k_, v_, ki_, c_, gv_) = trunk_layer(ys, attn_s, state_conv[l].astype(ys.dtype), *lw)
        kss.append(k_)
        vss.append(v_)
        kis.append(ki_)
        css.append(c_)
        gvs.append(gv_)
    return (yp, ys, jnp.stack(kp), jnp.stack(vp), jnp.stack(kip), jnp.stack(cp),
            jnp.stack(kss), jnp.stack(vss), jnp.stack(kis), jnp.stack(css), jnp.stack(gvs))
```

```python
import functools
import math

import numpy as np
import jax
import jax.numpy as jnp
from jax import lax
from jax.experimental import pallas as pl
from jax.experimental.pallas import tpu as pltpu

F32 = jnp.float32
BF = jnp.bfloat16

N_HEADS = 8
N_KV_HEADS = 2
HEAD_DIM = 64
GROUP_SIZE = N_HEADS // N_KV_HEADS
N_IDX_HEADS = 8
IDX_DIM = 64
TOPK_MAX = 256
N_BUCKETS = 32
MAX_DISTANCE = 128
N_GMLP_GROUPS = 4
GMLP_CHUNK = 128
EPS = 1e-6

LANES = 128
MASKED = -1e30
M_INIT = -1e29
F32_LOWEST = -3.4028234663852886e38
VMEM_LIMIT = 56 * 1024 * 1024

NT_DIMS = (((1,), (1,)), ((), ()))


def _cparams(n_axes):
    return pltpu.CompilerParams(dimension_semantics=("arbitrary",) * n_axes,
                                vmem_limit_bytes=VMEM_LIMIT)


def _resident(shape, layer=None):
    nd = len(shape)
    if layer is None:
        return pl.BlockSpec(shape, lambda *_: (0,) * nd, pipeline_mode=pl.Buffered(1))
    return pl.BlockSpec((None,) + tuple(shape), lambda *_: (layer,) + (0,) * nd,
                        pipeline_mode=pl.Buffered(1))


def _rms(x, g):
    return x * lax.rsqrt(jnp.mean(x * x, axis=-1, keepdims=True) + EPS) * g


def _layer_norm(x, g, b):
    mu = jnp.mean(x, axis=-1, keepdims=True)
    xc = x - mu
    return xc * lax.rsqrt(jnp.mean(xc * xc, axis=-1, keepdims=True) + EPS) * g + b


def _dot(a, b):
    return jnp.dot(a, b, preferred_element_type=F32)


def _ffn_body(x_ref, pre_ref, post_ref, wg_ref, wu_ref, wd_ref, o_ref, *, ff_chunk):
    x = x_ref[...]
    h = _rms(x, pre_ref[...]).astype(BF)
    acc = jnp.zeros(x.shape, F32)
    for c in range(wg_ref.shape[1] // ff_chunk):
        sl = slice(c * ff_chunk, (c + 1) * ff_chunk)
        a = jax.nn.silu(_dot(h, wg_ref[:, sl])) * _dot(h, wu_ref[:, sl])
        acc = acc + _dot(a.astype(BF), wd_ref[sl, :])
    o_ref[...] = x + 0.5 * _rms(acc, post_ref[...])


def _ffn(x, layer, pre, post, wg, wu, wd, *, tm):
    n, d = x.shape
    dff = wg.shape[2]
    return pl.pallas_call(
        functools.partial(_ffn_body, ff_chunk=256),
        out_shape=jax.ShapeDtypeStruct((n, d), F32),
        grid=(n // tm,),
        in_specs=[pl.BlockSpec((tm, d), lambda i: (i, 0)),
                  _resident((1, d), layer), _resident((1, d), layer),
                  _resident((d, dff), layer), _resident((d, dff), layer), _resident((dff, d), layer)],
        out_specs=pl.BlockSpec((tm, d), lambda i: (i, 0)),
        compiler_params=_cparams(1),
        name="ffn",
    )(x, pre, post, wg, wu, wd)


def _mix_in_body(*refs, tm, conv_w, sample, n_seq):
    (x_ref, pre_ref, wa_ref, wq_ref, wkv_ref, wkd_ref, wvd_ref, wqi_ref, wkw_ref, wkk_ref,
     wuv_ref, wg_ref, cw_ref, cb_ref, clg_ref, clb_ref, glg_ref, glb_ref, ws_ref, bsb_ref) = refs[:20]
    if sample:
        st_ref = refs[20]
        outs = refs[21:33]
        xin_ref = refs[33]
    else:
        outs = refs[20:31]
        xin_ref = refs[31]
    (q_ref, kv_ref, kd_ref, vd_ref, qi_ref, kw_ref, kk_ref, ca_ref, gm_ref, gt_ref, cs_ref) = outs[:11]

    h = _rms(x_ref[...], pre_ref[...]).astype(BF)

    q_ref[...] = (_dot(h, wq_ref[...]) * (HEAD_DIM ** -0.5)).astype(BF)
    kv_ref[...] = _dot(h, wkv_ref[...])
    kd_ref[...] = _dot(h, wkd_ref[...]).astype(BF)
    vd_ref[...] = _dot(h, wvd_ref[...]).astype(BF)
    qi_ref[...] = _dot(h, wqi_ref[...]).astype(BF)
    kw_ref[...] = _dot(h, wkw_ref[...])
    kk_ref[...] = _dot(h, wkk_ref[...]).astype(BF)
    gt_ref[...] = jax.nn.sigmoid(_dot(h, wg_ref[...])).astype(BF)

    a = _dot(h, wa_ref[...])
    cc = a.shape[1] // 2
    glu = a[:, :cc] * jax.nn.sigmoid(a[:, cc:])
    pre_rows = conv_w - 1
    if sample:
        n_t = tm // n_seq
        xin_ref[0:pre_rows] = st_ref[...]
        for t in range(n_t):
            xin_ref[pre_rows + t] = glu[t * n_seq:(t + 1) * n_seq, :]
        ys = []
        for t in range(n_t):
            y = jnp.zeros((n_seq, cc), F32)
            for j in range(conv_w):
                y = y + xin_ref[t + j] * cw_ref[j:j + 1, :]
            ys.append(y)
        y = jnp.concatenate(ys, axis=0)
        cs_ref[...] = xin_ref[n_t:n_t + pre_rows]
        ca_ref[...] = jax.nn.silu(_layer_norm(y + cb_ref[...], clg_ref[...], clb_ref[...])).astype(BF)
    else:
        halo = 32
        @pl.when(pl.program_id(1) == 0)
        def _():
            xin_ref[0:halo, :] = jnp.zeros((halo, cc), F32)
        xin_ref[halo:halo + tm, :] = glu
        rb = 64
        off = halo - pre_rows
        for r0 in range(0, tm, rb):
            y = jnp.zeros((rb, cc), F32)
            for j in range(conv_w):
                y = y + xin_ref[r0 + off + j:r0 + off + j + rb, :] * cw_ref[j:j + 1, :]
            y = jax.nn.silu(_layer_norm(y + cb_ref[...], clg_ref[...], clb_ref[...]))
            ca_ref[r0:r0 + rb, :] = y.astype(BF)
        cs_ref[...] = xin_ref[halo + tm - pre_rows:halo + tm, :]
        xin_ref[0:halo, :] = xin_ref[tm:tm + halo, :]

    uv = jax.nn.gelu(_dot(h, wuv_ref[...]))
    cg = uv.shape[1] // 2
    u = uv[:, :cg]
    vn = _layer_norm(uv[:, cg:], glg_ref[...], glb_ref[...])
    if sample:
        outs[11][...] = vn
    vb = vn.astype(BF)
    gd = cg // N_GMLP_GROUPS
    chunk = ws_ref.shape[-1]
    ri = lax.broadcasted_iota(jnp.int32, (chunk, chunk), 0)
    ci = lax.broadcasted_iota(jnp.int32, (chunk, chunk), 1)
    wsm = [jnp.where(ci <= ri, ws_ref[g], 0.0).astype(BF) for g in range(N_GMLP_GROUPS)]
    for n in range(tm // chunk):
        rs = slice(n * chunk, (n + 1) * chunk)
        mixed = jnp.concatenate(
            [_dot(wsm[g], vb[rs, g * gd:(g + 1) * gd]) for g in range(N_GMLP_GROUPS)], axis=1)
        gm_ref[rs, :] = (u[rs, :] * (mixed + bsb_ref[...])).astype(BF)


def _mix_in(x, layer, w, *, tm, n_seq, seq_tiles, sample, state=None):
    n, d = x.shape
    conv_w = w["conv_w"].shape[1]
    cc = w["conv_w"].shape[2]
    cg = w["gmlp_ln_g"].shape[2]
    pre_rows = conv_w - 1
    if sample:
        grid = (1,)
        row = lambda i: (0, 0)
        res = lambda shape: _resident(shape, layer)
        cs_shape = (pre_rows, n_seq, cc)
        cs_spec = pl.BlockSpec(cs_shape, lambda i: (0, 0, 0))
        xin = pltpu.VMEM((pre_rows + tm // n_seq, n_seq, cc), F32)
    else:
        grid = (n_seq, seq_tiles)
        row = lambda b, t: (b * seq_tiles + t, 0)
        res = lambda shape: _resident(shape, layer)
        cs_shape = (n_seq, pre_rows, cc)
        cs_spec = pl.BlockSpec((None, pre_rows, cc), lambda b, t: (b, 0, 0))
        xin = pltpu.VMEM((32 + tm, cc), F32)

    wnames = ["wa", "wq", "wkv", "wkd", "wvd", "wqi", "wkw", "wkk", "wuv", "wg"]
    in_specs = [pl.BlockSpec((tm, d), row), res((1, d))]
    in_specs += [res(w[k].shape[1:]) for k in wnames]
    in_specs += [res((conv_w, cc)), res((1, cc)), res((1, cc)), res((1, cc)),
                 res((1, cg)), res((1, cg)),
                 res(w["ws"].shape[1:]), res(w["bsb"].shape[1:])]
    args = [x, w["mix_pre"]] + [w[k] for k in wnames]
    args += [w["conv_w"], w["conv_b"], w["conv_ln_g"], w["conv_ln_b"], w["gmlp_ln_g"], w["gmlp_ln_b"],
             w["ws"], w["bsb"]]
    if sample:
        in_specs.append(pl.BlockSpec(cs_shape, lambda i: (0, 0, 0)))
        args.append(state)

    widths = [("q", 512, BF), ("kv", 256, F32), ("kd", 256, BF), ("vd", 256, BF), ("qi", 512, BF),
              ("kw", 128, F32), ("kk", 128, BF), ("ca", cc, BF), ("gm", cg, BF),
              ("gates", w["wg"].shape[2], BF)]
    out_shape = [jax.ShapeDtypeStruct((n, wd), dt) for _, wd, dt in widths]
    out_specs = [pl.BlockSpec((tm, wd), row) for _, wd, _ in widths]
    out_shape.append(jax.ShapeDtypeStruct(cs_shape, F32))
    out_specs.append(cs_spec)
    if sample:
        out_shape.append(jax.ShapeDtypeStruct((n, cg), F32))
        out_specs.append(pl.BlockSpec((tm, cg), row))

    outs = pl.pallas_call(
        functools.partial(_mix_in_body, tm=tm, conv_w=conv_w, sample=sample, n_seq=n_seq),
        out_shape=out_shape, grid=grid, in_specs=in_specs, out_specs=out_specs,
        scratch_shapes=[xin], compiler_params=_cparams(len(grid)),
        name="mix_in_sample" if sample else "mix_in",
    )(*args)
    names = [nm for nm, _, _ in widths] + ["conv_state"] + (["gmlp_v"] if sample else [])
    return dict(zip(names, outs))


def _ordered_bits_to_float(u):
    t = u ^ jnp.int32(-2 ** 31)
    fb = t ^ (lax.shift_right_arithmetic(t, 31) & jnp.int32(0x7FFFFFFF))
    return lax.bitcast_convert_type(fb, F32)


def _row_total(cnt, ones_bf):
    return _dot(cnt.astype(BF), ones_bf)


def _select_topk(sc_ref, nk, rows, width, k_sel):
    nh = width // LANES
    ones_bf = jnp.ones((LANES, LANES), BF)
    kf = float(k_sel)

    def halves(x):
        return [x[:, i * LANES:(i + 1) * LANES] for i in range(nh)]

    def count(pred):
        def body(c, cnt):
            for xh in halves(sc_ref[c]):
                cnt = cnt + jnp.where(pred(xh), 1.0, 0.0)
            return cnt
        return _row_total(lax.fori_loop(0, nk, body, jnp.zeros((rows, LANES), F32)), ones_bf)

    def bit_body(i, u):
        cand = u | jnp.left_shift(jnp.int32(1), 31 - i)
        thr = _ordered_bits_to_float(cand)
        return jnp.where(count(lambda xh: xh >= thr) >= kf, cand, u)

    u = lax.fori_loop(0, 32, bit_body, jnp.zeros((rows, LANES), jnp.int32))
    thr = _ordered_bits_to_float(u)
    thr = jnp.where(thr >= F32_LOWEST, thr, F32_LOWEST)

    n_ge = count(lambda xh: xh >= thr)

    @pl.when(jnp.max(n_ge) > kf)
    def _():
        need = kf - count(lambda xh: xh > thr)
        ri = lax.broadcasted_iota(jnp.int32, (LANES, LANES), 0)
        ci = lax.broadcasted_iota(jnp.int32, (LANES, LANES), 1)
        before = jnp.where(ri < ci, 1.0, 0.0).astype(BF)

        def fix(c, seen):
            out = []
            for xh in halves(sc_ref[c]):
                tie = xh == thr
                tf = jnp.where(tie, 1.0, 0.0).astype(BF)
                rank = seen + _dot(tf, before)
                out.append(jnp.where(jnp.where(tie, rank, -1.0) >= need, -jnp.inf, xh))
                seen = seen + _dot(tf, ones_bf)
            sc_ref[c] = out[0] if nh == 1 else jnp.concatenate(out, axis=1)
            return seen

        lax.fori_loop(0, nk, fix, jnp.zeros((rows, LANES), F32))

    def to_mask(c, carry):
        x = sc_ref[c]
        thr_w = thr if nh == 1 else jnp.concatenate([thr] * nh, axis=1)
        sc_ref[c] = jnp.where(x >= thr_w, 0.0, MASKED)
        return carry

    lax.fori_loop(0, nk, to_mask, 0)


def _online_softmax_step(s, v, m, l, acc):
    m_new = jnp.maximum(m, jnp.max(s, axis=1, keepdims=True))
    alpha = jnp.exp(m - m_new)
    p = jnp.exp(s - m_new)
    l = alpha * l + jnp.sum(p, axis=1, keepdims=True)
    acc = alpha * acc + _dot(p.astype(BF), v)
    return m_new, l, acc


def _attn_prompt_body(q_ref, qi_ref, kw_ref, kk_ref, kd_ref, vd_ref, band_ref, o_ref,
                      sc_ref, wb_ref, *, tq, k_sel):
    j = pl.program_id(1)
    nk = j + 1
    lane = lax.broadcasted_iota(jnp.int32, (1, LANES), 1)
    half_mask = [jnp.where(lane < IDX_DIM, 1.0, 0.0).astype(BF),
                 jnp.where(lane >= IDX_DIM, 1.0, 0.0).astype(BF)]

    kw = kw_ref[...]
    for h in range(N_IDX_HEADS):
        wb_ref[h] = jnp.broadcast_to(kw[:, IDX_DIM + h:IDX_DIM + h + 1], (tq, LANES))
    qi = qi_ref[...]
    qih = [qi[:, (h // 2) * LANES:(h // 2 + 1) * LANES] * half_mask[h % 2] for h in range(N_IDX_HEADS)]
    q_pos = lax.broadcasted_iota(jnp.int32, (tq, tq), 0) + j * tq

    def score_chunk(c, carry):
        k0 = pl.multiple_of(c * tq, tq)
        kic = kk_ref[pl.ds(k0, tq), :]
        acc = jnp.zeros((tq, tq), F32)
        for h in range(N_IDX_HEADS):
            s = lax.dot_general(qih[h], kic, NT_DIMS, preferred_element_type=F32)
            wgt = jnp.concatenate([wb_ref[h]] * (tq // LANES), axis=1)
            acc = acc + jnp.maximum(s, 0.0) * wgt
        k_pos = lax.broadcasted_iota(jnp.int32, (tq, tq), 1) + k0
        sc_ref[c] = jnp.where(k_pos <= q_pos, acc, -jnp.inf)
        return carry

    lax.fori_loop(0, nk, score_chunk, 0)
    _select_topk(sc_ref, nk, tq, tq, k_sel)

    q = q_ref[...]
    lane_f = lax.broadcasted_iota(jnp.int32, (tq, LANES), 1)
    j_prev = jnp.maximum(j - 1, 0)
    prev_mask = jnp.where(j > 0, 0.0, MASKED)
    o_heads = []
    for h in range(N_HEADS):
        g = h // GROUP_SIZE
        gs = slice(g * LANES, (g + 1) * LANES)
        qh = q[:, (h // 2) * LANES:(h // 2 + 1) * LANES] * half_mask[h % 2]

        def logits(c):
            k0 = pl.multiple_of(c * tq, tq)
            s = lax.dot_general(qh, kd_ref[pl.ds(k0, tq), gs], NT_DIMS, preferred_element_type=F32)
            return s + sc_ref[c], vd_ref[pl.ds(k0, tq), gs]

        def far(c, carry):
            s, v = logits(c)
            return _online_softmax_step(s, v, *carry)

        carry = (jnp.full((tq, 1), M_INIT, F32), jnp.zeros((tq, 1), F32), jnp.zeros((tq, LANES), F32))
        carry = lax.fori_loop(0, j_prev, far, carry)
        s, v = logits(j_prev)
        carry = _online_softmax_step(s + band_ref[h, :, 0:tq] + prev_mask, v, *carry)
        s, v = logits(j)
        m, l, acc = _online_softmax_step(s + band_ref[h, :, tq:2 * tq], v, *carry)
        o_heads.append(acc / l)
    for p in range(N_HEADS // 2):
        o_ref[:, p * LANES:(p + 1) * LANES] = jnp.where(
            lane_f < HEAD_DIM, o_heads[2 * p], o_heads[2 * p + 1]).astype(BF)


def _attn_prompt(mi, band, *, n_seq, seq, tq, k_sel):
    nt = seq // tq
    row = lambda b, t: (b * nt + t, 0)
    per_seq = lambda b, t: (b, 0)
    return pl.pallas_call(
        functools.partial(_attn_prompt_body, tq=tq, k_sel=k_sel),
        out_shape=jax.ShapeDtypeStruct((n_seq * seq, N_HEADS * HEAD_DIM), BF),
        grid=(n_seq, nt),
        in_specs=[pl.BlockSpec((tq, 512), row), pl.BlockSpec((tq, 512), row), pl.BlockSpec((tq, 128), row),
                  pl.BlockSpec((seq, 128), per_seq), pl.BlockSpec((seq, 256), per_seq),
                  pl.BlockSpec((seq, 256), per_seq), _resident(band.shape)],
        out_specs=pl.BlockSpec((tq, N_HEADS * HEAD_DIM), row),
        scratch_shapes=[pltpu.VMEM((nt, tq, tq), F32), pltpu.VMEM((N_IDX_HEADS, tq, LANES), F32)],
        compiler_params=_cparams(2),
        name="attn_prompt",
    )(mi["q"], mi["qi"], mi["kw"], mi["kk"], mi["kd"], mi["vd"], band)


def _attn_sample_body(pt_ref, qi_ref, wq_ref, q_ref, kin_ref, kn_ref, vn_ref, band_ref,
                      cki_ref, ck_ref, cv_ref, o_ref,
                      kib_ref, kb_ref, vb_ref, sc_ref, sem_ref, *, layer, n_pages, page, n_new, k_sel):
    b = pl.program_id(0)
    rows_t = 8
    rows = N_HEADS * rows_t

    def page_copies(p):
        phys = pt_ref[b, p]
        return (pltpu.make_async_copy(cki_ref.at[layer, phys], kib_ref.at[p], sem_ref.at[0]),
                pltpu.make_async_copy(ck_ref.at[layer, phys], kb_ref.at[p], sem_ref.at[1]),
                pltpu.make_async_copy(cv_ref.at[layer, phys], vb_ref.at[p], sem_ref.at[2]))

    def start(p, carry):
        for cp in page_copies(p):
            cp.start()
        return carry

    lax.fori_loop(0, n_pages, start, 0)

    kib_ref[n_pages] = jnp.zeros(kib_ref.shape[1:], F32)
    kb_ref[n_pages] = jnp.zeros(kb_ref.shape[1:], F32)
    vb_ref[n_pages] = jnp.zeros(vb_ref.shape[1:], F32)
    kib_ref[n_pages, 0:n_new, :] = kin_ref[...]
    kb_ref[n_pages, 0:n_new, :] = kn_ref[...]
    vb_ref[n_pages, 0:n_new, :] = vn_ref[...]

    def wait(p, carry):
        for cp in page_copies(p):
            cp.wait()
        return carry

    lax.fori_loop(0, n_pages, wait, 0)

    nk = n_pages + 1
    past = n_pages * page
    qi = qi_ref[...]
    wq = wq_ref[...]
    q_pos = lax.broadcasted_iota(jnp.int32, (rows_t, page), 0) + past

    def score_chunk(c, carry):
        s = lax.dot_general(qi, kib_ref[c].astype(BF), NT_DIMS, preferred_element_type=F32)
        s = (jnp.maximum(s, 0.0) * wq).reshape(N_IDX_HEADS, rows_t, page)
        acc = s[0]
        for h in range(1, N_IDX_HEADS):
            acc = acc + s[h]
        k_pos = lax.broadcasted_iota(jnp.int32, (rows_t, page), 1) + c * page
        sc_ref[c] = jnp.where(k_pos <= q_pos, acc, -jnp.inf)
        return carry

    lax.fori_loop(0, nk, score_chunk, 0)
    _select_topk(sc_ref, nk, rows_t, page, k_sel)

    q = q_ref[...]

    def logits(c):
        s = lax.dot_general(q, kb_ref[c].astype(BF), NT_DIMS, preferred_element_type=F32)
        s = s.reshape(N_HEADS, rows_t, page) + sc_ref[c][None]
        return s.reshape(rows, page), vb_ref[c].astype(BF)

    def far(c, carry):
        s, v = logits(c)
        return _online_softmax_step(s, v, *carry)

    carry = (jnp.full((rows, 1), M_INIT, F32), jnp.zeros((rows, 1), F32), jnp.zeros((rows, LANES), F32))
    carry = lax.fori_loop(0, n_pages - 1, far, carry)
    s, v = logits(n_pages - 1)
    carry = _online_softmax_step(s + band_ref[:, 0:page], v, *carry)
    s, v = logits(n_pages)
    m, l, acc = _online_softmax_step(s + band_ref[:, page:2 * page], v, *carry)
    o_ref[...] = acc / l


def _attn_sample(page_table, qi_rows, wq_rows, q_rows, ki_new, k_new, v_new, band, cache_idx_k, cache_k,
                 cache_v, *, layer, k_sel):
    n_seq, n_pages = page_table.shape
    page = cache_k.shape[2]
    n_new = ki_new.shape[1]
    rows = q_rows.shape[1]
    per_seq3 = lambda b, pt: (b, 0, 0)
    any_spec = pl.BlockSpec(memory_space=pl.ANY)
    grid_spec = pltpu.PrefetchScalarGridSpec(
        num_scalar_prefetch=1,
        grid=(n_seq,),
        in_specs=[pl.BlockSpec((None, rows, IDX_DIM), per_seq3),
                  pl.BlockSpec((None, rows, LANES), per_seq3),
                  pl.BlockSpec((None, rows, LANES), per_seq3),
                  pl.BlockSpec((None, n_new, IDX_DIM), per_seq3),
                  pl.BlockSpec((None, n_new, LANES), per_seq3),
                  pl.BlockSpec((None, n_new, LANES), per_seq3),
                  pl.BlockSpec(band.shape, lambda b, pt: (0, 0), pipeline_mode=pl.Buffered(1)),
                  any_spec, any_spec, any_spec],
        out_specs=pl.BlockSpec((None, rows, LANES), per_seq3),
        scratch_shapes=[pltpu.VMEM((n_pages + 1, page, IDX_DIM), F32),
                        pltpu.VMEM((n_pages + 1, page, LANES), F32),
                        pltpu.VMEM((n_pages + 1, page, LANES), F32),
                        pltpu.VMEM((n_pages + 1, 8, page), F32),
                        pltpu.SemaphoreType.DMA((3,))],
    )
    return pl.pallas_call(
        functools.partial(_attn_sample_body, layer=layer, n_pages=n_pages, page=page, n_new=n_new,
                          k_sel=k_sel),
        out_shape=jax.ShapeDtypeStruct((n_seq, rows, LANES), F32),
        grid_spec=grid_spec,
        compiler_params=_cparams(1),
        name="attn_sample",
    )(page_table, qi_rows, wq_rows, q_rows, ki_new, k_new, v_new, band, cache_idx_k, cache_k, cache_v)


def _merge_body(x_ref, ca_ref, att_ref, gm_ref, gt_ref, wco_ref, wao_ref, wgo_ref, wout_ref, post_ref, o_ref):
    d = x_ref.shape[1]
    gt = gt_ref[...].astype(F32)
    m = (gt[:, 0:d] * _dot(ca_ref[...], wco_ref[...])
         + gt[:, d:2 * d] * _dot(att_ref[...], wao_ref[...])
         + gt[:, 2 * d:3 * d] * _dot(gm_ref[...], wgo_ref[...]))
    o_ref[...] = x_ref[...] + _rms(_dot(m.astype(BF), wout_ref[...]), post_ref[...])


def _merge(x, ca, att, gm, gates, layer, w, *, tm):
    n, d = x.shape
    row = lambda i: (i, 0)
    return pl.pallas_call(
        _merge_body,
        out_shape=jax.ShapeDtypeStruct((n, d), F32),
        grid=(n // tm,),
        in_specs=[pl.BlockSpec((tm, d), row), pl.BlockSpec((tm, ca.shape[1]), row),
                  pl.BlockSpec((tm, att.shape[1]), row), pl.BlockSpec((tm, gm.shape[1]), row),
                  pl.BlockSpec((tm, gates.shape[1]), row),
                  _resident(w["wco"].shape[1:], layer), _resident(w["wao"].shape[1:], layer),
                  _resident(w["wgo"].shape[1:], layer), _resident(w["wout"].shape[1:], layer),
                  _resident((1, d), layer)],
        out_specs=pl.BlockSpec((tm, d), row),
        compiler_params=_cparams(1),
        name="merge",
    )(x, ca, att, gm, gates, w["wco"], w["wao"], w["wgo"], w["wout"], w["mix_post"])


def _band_body(rb_ref, bucket_ref, o_ref):
    h = pl.program_id(0)
    bucket = bucket_ref[...]
    far = rb_ref[N_BUCKETS - 1, h]
    tile = jnp.zeros(bucket.shape, F32)
    for bkt in range(N_BUCKETS):
        tile = jnp.where(bucket == bkt, rb_ref[bkt, h] - far, tile)
    o_ref[...] = tile


def _band_bias(rel_bias, dist):
    max_exact = N_BUCKETS // 2
    d = jnp.maximum(dist, 1).astype(F32)
    large = max_exact + (jnp.log(d / max_exact) / math.log(MAX_DISTANCE / max_exact)
                         * (N_BUCKETS - max_exact)).astype(jnp.int32)
    bucket = jnp.where(dist < max_exact, dist, jnp.minimum(large, N_BUCKETS - 1))
    r, c = dist.shape
    return pl.pallas_call(
        _band_body,
        out_shape=jax.ShapeDtypeStruct((N_HEADS, r, c), F32),
        grid=(N_HEADS,),
        in_specs=[pl.BlockSpec(memory_space=pltpu.SMEM), pl.BlockSpec((r, c), lambda h: (0, 0))],
        out_specs=pl.BlockSpec((None, r, c), lambda h: (h, 0, 0)),
        compiler_params=_cparams(1),
        name="band_bias",
    )(rel_bias, bucket)


def _prep_weights(p):
    d = p["w_in"].shape[1]
    cc = p["conv_w"].shape[2]
    cg = p["gmlp_ln_g"].shape[1]
    hq = N_HEADS * HEAD_DIM
    hk = N_KV_HEADS * HEAD_DIM
    hi = N_IDX_HEADS * IDX_DIM
    win = p["w_in"].astype(BF)
    o = 0
    cols = {}
    for name, wd in (("a", 2 * cc), ("q", hq), ("k", hk), ("v", hk), ("qi", hi), ("ki", IDX_DIM),
                     ("wi", N_IDX_HEADS), ("uv", 2 * cg), ("g", 3 * d)):
        cols[name] = win[:, :, o:o + wd]
        o += wd
    k0, k1 = cols["k"][..., :HEAD_DIM], cols["k"][..., HEAD_DIM:]
    v0, v1 = cols["v"][..., :HEAD_DIM], cols["v"][..., HEAD_DIM:]
    pad = jnp.zeros(cols["wi"].shape[:2] + (LANES - IDX_DIM - N_IDX_HEADS,), BF)
    vec = lambda a: a[:, None, :]
    return {
        "wa": cols["a"], "wq": cols["q"],
        "wkv": jnp.concatenate([cols["k"], cols["v"]], axis=-1),
        "wkd": jnp.concatenate([k0, k0, k1, k1], axis=-1),
        "wvd": jnp.concatenate([v0, v0, v1, v1], axis=-1),
        "wqi": cols["qi"],
        "wkw": jnp.concatenate([cols["ki"], cols["wi"], pad], axis=-1),
        "wkk": jnp.concatenate([cols["ki"], cols["ki"]], axis=-1),
        "wuv": cols["uv"], "wg": cols["g"],
        "mix_pre": vec(p["mix_norm_pre"]), "mix_post": vec(p["mix_norm_post"]),
        "conv_w": p["conv_w"], "conv_b": vec(p["conv_b"]),
        "conv_ln_g": vec(p["conv_ln_g"]), "conv_ln_b": vec(p["conv_ln_b"]),
        "gmlp_ln_g": vec(p["gmlp_ln_g"]), "gmlp_ln_b": vec(p["gmlp_ln_b"]),
        "wco": p["w_conv_out"].astype(BF), "wao": p["w_attn_out"].astype(BF),
        "wgo": p["w_gmlp_out"].astype(BF), "wout": p["w_out"].astype(BF),
        "f1": (vec(p["ffn1_norm_pre"]), vec(p["ffn1_norm_post"]), p["ffn1_w_gate"].astype(BF),
               p["ffn1_w_up"].astype(BF), p["ffn1_w_down"].astype(BF)),
        "f2": (vec(p["ffn2_norm_pre"]), vec(p["ffn2_norm_post"]), p["ffn2_w_gate"].astype(BF),
               p["ffn2_w_up"].astype(BF), p["ffn2_w_down"].astype(BF)),
    }


def _gmlp_spatial(ws, bs, rows, n_seq):
    gd = LANES
    if n_seq is None:
        w = ws
        b = jnp.swapaxes(bs, 1, 2)
    else:
        t = rows // n_seq
        eye = jnp.eye(n_seq, dtype=ws.dtype)
        w = jnp.einsum("lgts,bc->lgtbsc", ws[:, :, :t, :t], eye).reshape(ws.shape[0], ws.shape[1], rows, rows)
        b = jnp.repeat(jnp.swapaxes(bs[:, :, :t], 1, 2), n_seq, axis=1)
    return w, jnp.repeat(b, gd, axis=2)


def kernel(x_prompt, x_sample, cache_k, cache_v, cache_idx_k, state_conv, page_table,
           ffn1_norm_pre, ffn1_norm_post, ffn1_w_gate, ffn1_w_up, ffn1_w_down,
           mix_norm_pre, mix_norm_post, w_in, conv_w, conv_b, conv_ln_g, conv_ln_b,
           w_conv_out, w_attn_out, rel_bias, gmlp_ln_g, gmlp_ln_b, gmlp_ws, gmlp_bs,
           w_gmlp_out, w_out, ffn2_norm_pre, ffn2_norm_post, ffn2_w_gate, ffn2_w_up, ffn2_w_down):
    params = dict(ffn1_norm_pre=ffn1_norm_pre, ffn1_norm_post=ffn1_norm_post, ffn1_w_gate=ffn1_w_gate,
                  ffn1_w_up=ffn1_w_up, ffn1_w_down=ffn1_w_down, mix_norm_pre=mix_norm_pre,
                  mix_norm_post=mix_norm_post, w_in=w_in, conv_w=conv_w, conv_b=conv_b,
                  conv_ln_g=conv_ln_g, conv_ln_b=conv_ln_b, w_conv_out=w_conv_out, w_attn_out=w_attn_out,
                  gmlp_ln_g=gmlp_ln_g, gmlp_ln_b=gmlp_ln_b, w_gmlp_out=w_gmlp_out, w_out=w_out,
                  ffn2_norm_pre=ffn2_norm_pre, ffn2_norm_post=ffn2_norm_post, ffn2_w_gate=ffn2_w_gate,
                  ffn2_w_up=ffn2_w_up, ffn2_w_down=ffn2_w_down)
    depth = w_in.shape[0]
    nb, seq, d = x_prompt.shape
    db, t_new, _ = x_sample.shape
    page = cache_k.shape[2]
    n_pages = page_table.shape[1]
    past = n_pages * page
    n_s = db * t_new
    cc = conv_w.shape[2]
    cg = gmlp_ln_g.shape[1]

    w = _prep_weights(params)
    wp = dict(w)
    wp["ws"], wp["bsb"] = _gmlp_spatial(gmlp_ws, gmlp_bs, GMLP_CHUNK, None)
    wsm = dict(w)
    wsm["ws"], wsm["bsb"] = _gmlp_spatial(gmlp_ws, gmlp_bs, n_s, db)

    tq = 256
    tm_p = 512 if seq % 512 == 0 else 256
    k_sel_p = min(TOPK_MAX, seq // 4)
    k_sel_s = min(TOPK_MAX, (past + t_new) // 4)

    r = jnp.arange(tq, dtype=jnp.int32)[:, None]
    c = jnp.arange(2 * tq, dtype=jnp.int32)[None, :]
    band_p = _band_bias(rel_bias, jnp.maximum(tq + r - c, 0))
    r8 = jnp.minimum(jnp.arange(8, dtype=jnp.int32), t_new - 1)[:, None]
    c2 = jnp.arange(2 * page, dtype=jnp.int32)[None, :]
    band_s = _band_bias(rel_bias, jnp.maximum(r8 + page - c2, 0)).reshape(N_HEADS * 8, 2 * page)

    cache_k2 = cache_k.reshape(cache_k.shape[:3] + (N_KV_HEADS * HEAD_DIM,))
    cache_v2 = cache_v.reshape(cache_v.shape[:3] + (N_KV_HEADS * HEAD_DIM,))
    state_t = jnp.swapaxes(state_conv, 1, 2)

    xp = x_prompt.reshape(nb * seq, d)
    xs = jnp.swapaxes(x_sample, 0, 1).reshape(n_s, d)

    def heads_rows(a, width):
        a = a.reshape(t_new, db, -1, width).transpose(1, 2, 0, 3)
        a = jnp.pad(a, ((0, 0), (0, 0), (0, 8 - t_new), (0, 0)), mode="edge")
        return a.reshape(db, -1, width)

    outs = {k: [] for k in ("kp", "vp", "kip", "cp", "ks", "vs", "kis", "cs", "gv")}
    for l in range(depth):
        xp = _ffn(xp, l, *w["f1"], tm=tm_p)
        mi = _mix_in(xp, l, wp, tm=tm_p, n_seq=nb, seq_tiles=seq // tm_p, sample=False)
        att = _attn_prompt(mi, band_p, n_seq=nb, seq=seq, tq=tq, k_sel=k_sel_p)
        xp = _merge(xp, mi["ca"], att, mi["gm"], mi["gates"], l, w, tm=tm_p)
        xp = _ffn(xp, l, *w["f2"], tm=tm_p)
        kv = mi["kv"].reshape(nb, seq, 2, N_KV_HEADS, HEAD_DIM)
        outs["kp"].append(kv[:, :, 0])
        outs["vp"].append(kv[:, :, 1])
        outs["kip"].append(mi["kw"][:, :IDX_DIM].reshape(nb, seq, IDX_DIM))
        outs["cp"].append(mi["conv_state"])

        xs = _ffn(xs, l, *w["f1"], tm=n_s)
        ms = _mix_in(xs, l, wsm, tm=n_s, n_seq=db, seq_tiles=1, sample=True, state=state_t[l])
        kv_s = jnp.swapaxes(ms["kv"].reshape(t_new, db, 2, N_KV_HEADS * HEAD_DIM), 0, 1)
        ki_s = jnp.swapaxes(ms["kw"][:, :IDX_DIM].reshape(t_new, db, IDX_DIM), 0, 1)
        qi_rows = heads_rows(ms["qi"], IDX_DIM)
        wq_rows = heads_rows(jnp.broadcast_to(
            ms["kw"][:, IDX_DIM:IDX_DIM + N_IDX_HEADS, None], (n_s, N_IDX_HEADS, LANES)
        ).reshape(n_s, N_IDX_HEADS * LANES), LANES)
        qh = heads_rows(ms["q"], HEAD_DIM)
        zero = jnp.zeros_like(qh)
        first = (jnp.arange(N_HEADS * 8) < GROUP_SIZE * 8)[None, :, None]
        q_rows = jnp.concatenate([jnp.where(first, qh, zero), jnp.where(first, zero, qh)], axis=-1)
        o = _attn_sample(page_table, qi_rows, wq_rows, q_rows, ki_s, kv_s[:, :, 0], kv_s[:, :, 1], band_s,
                         cache_idx_k, cache_k2, cache_v2, layer=l, k_sel=k_sel_s)
        o = o.reshape(db, N_HEADS, 8, N_KV_HEADS, HEAD_DIM)[:, :, :t_new]
        o = jnp.concatenate([o[:, :GROUP_SIZE, :, 0], o[:, GROUP_SIZE:, :, 1]], axis=1)
        att_s = o.transpose(2, 0, 1, 3).reshape(n_s, N_HEADS * HEAD_DIM).astype(BF)
        xs = _merge(xs, ms["ca"], att_s, ms["gm"], ms["gates"], l, w, tm=n_s)
        xs = _ffn(xs, l, *w["f2"], tm=n_s)
        outs["ks"].append(kv_s[:, :, 0].reshape(db, t_new, N_KV_HEADS, HEAD_DIM))
        outs["vs"].append(kv_s[:, :, 1].reshape(db, t_new, N_KV_HEADS, HEAD_DIM))
        outs["kis"].append(ki_s)
        outs["cs"].append(jnp.swapaxes(ms["conv_state"], 0, 1))
        outs["gv"].append(jnp.swapaxes(ms["gmlp_v"].reshape(t_new, db, cg), 0, 1))

    yp = xp.reshape(nb, seq, d)
    ys = jnp.swapaxes(xs.reshape(t_new, db, d), 0, 1)
    st = lambda k: jnp.stack(outs[k])
    return (yp, ys, st("kp"), st("vp"), st("kip"), st("cp"),
            st("ks"), st("vs"), st("kis"), st("cs"), st("gv"))
```

```python
import functools
import math

import numpy as np
import jax
import jax.numpy as jnp
from jax import lax
from jax.experimental import pallas as pl
from jax.experimental.pallas import tpu as pltpu

F32 = jnp.float32
BF = jnp.bfloat16

N_HEADS = 8
N_KV_HEADS = 2
HEAD_DIM = 64
GROUP_SIZE = N_HEADS // N_KV_HEADS
N_IDX_HEADS = 8
IDX_DIM = 64
TOPK_MAX = 256
N_BUCKETS = 32
MAX_DISTANCE = 128
N_GMLP_GROUPS = 4
GMLP_CHUNK = 128
EPS = 1e-6

LANES = 128
MASKED = -1e30
M_INIT = -1e29
LOG2E = 1.4426950408889634
F32_LOWEST = -3.4028234663852886e38
VMEM_LIMIT = 56 * 1024 * 1024

NT_DIMS = (((1,), (1,)), ((), ()))


def _cparams(n_axes):
    return pltpu.CompilerParams(dimension_semantics=("arbitrary",) * n_axes,
                                vmem_limit_bytes=VMEM_LIMIT)


def _resident(shape, layer=None):
    nd = len(shape)
    if layer is None:
        return pl.BlockSpec(shape, lambda *_: (0,) * nd, pipeline_mode=pl.Buffered(1))
    return pl.BlockSpec((None,) + tuple(shape), lambda *_: (layer,) + (0,) * nd,
                        pipeline_mode=pl.Buffered(1))


def _rms(x, g):
    return x * lax.rsqrt(jnp.mean(x * x, axis=-1, keepdims=True) + EPS) * g


def _layer_norm(x, g, b):
    mu = jnp.mean(x, axis=-1, keepdims=True)
    xc = x - mu
    return xc * lax.rsqrt(jnp.mean(xc * xc, axis=-1, keepdims=True) + EPS) * g + b


def _dot(a, b):
    return jnp.dot(a, b, preferred_element_type=F32)


def _ffn_body(x_ref, pre_ref, post_ref, wg_ref, wu_ref, wd_ref, o_ref, *, ff_chunk):
    x = x_ref[...]
    h = _rms(x, pre_ref[...]).astype(BF)
    acc = jnp.zeros(x.shape, F32)
    for c in range(wg_ref.shape[1] // ff_chunk):
        sl = slice(c * ff_chunk, (c + 1) * ff_chunk)
        a = jax.nn.silu(_dot(h, wg_ref[:, sl])) * _dot(h, wu_ref[:, sl])
        acc = acc + _dot(a.astype(BF), wd_ref[sl, :])
    o_ref[...] = x + 0.5 * _rms(acc, post_ref[...])


def _ffn(x, layer, pre, post, wg, wu, wd, *, tm):
    n, d = x.shape
    dff = wg.shape[2]
    return pl.pallas_call(
        functools.partial(_ffn_body, ff_chunk=256),
        out_shape=jax.ShapeDtypeStruct((n, d), F32),
        grid=(n // tm,),
        in_specs=[pl.BlockSpec((tm, d), lambda i: (i, 0)),
                  _resident((1, d), layer), _resident((1, d), layer),
                  _resident((d, dff), layer), _resident((d, dff), layer), _resident((dff, d), layer)],
        out_specs=pl.BlockSpec((tm, d), lambda i: (i, 0)),
        compiler_params=_cparams(1),
        name="ffn",
    )(x, pre, post, wg, wu, wd)


def _mix_in_body(*refs, tm, conv_w, sample, n_seq):
    (x_ref, pre_ref, wa_ref, wq_ref, wkv_ref, wkd_ref, wvd_ref, wqi_ref, wkw_ref, wkk_ref,
     wuv_ref, wg_ref, cw_ref, cb_ref, clg_ref, clb_ref, glg_ref, glb_ref, ws_ref, bsb_ref) = refs[:20]
    if sample:
        st_ref = refs[20]
        outs = refs[21:33]
        xin_ref = refs[33]
    else:
        outs = refs[20:31]
        xin_ref = refs[31]
    (q_ref, kv_ref, kd_ref, vd_ref, qi_ref, kw_ref, kk_ref, ca_ref, gm_ref, gt_ref, cs_ref) = outs[:11]

    h = _rms(x_ref[...], pre_ref[...]).astype(BF)

    q_ref[...] = (_dot(h, wq_ref[...]) * (HEAD_DIM ** -0.5 * LOG2E)).astype(BF)
    kv_ref[...] = _dot(h, wkv_ref[...])
    kd_ref[...] = _dot(h, wkd_ref[...]).astype(BF)
    vd_ref[...] = _dot(h, wvd_ref[...]).astype(BF)
    qi_ref[...] = _dot(h, wqi_ref[...]).astype(BF)
    kw_ref[...] = _dot(h, wkw_ref[...])
    kk_ref[...] = _dot(h, wkk_ref[...]).astype(BF)
    gt_ref[...] = jax.nn.sigmoid(_dot(h, wg_ref[...])).astype(BF)

    a = _dot(h, wa_ref[...])
    cc = a.shape[1] // 2
    glu = a[:, :cc] * jax.nn.sigmoid(a[:, cc:])
    pre_rows = conv_w - 1
    if sample:
        n_t = tm // n_seq
        xin_ref[0:pre_rows] = st_ref[...]
        for t in range(n_t):
            xin_ref[pre_rows + t] = glu[t * n_seq:(t + 1) * n_seq, :]
        ys = []
        for t in range(n_t):
            y = jnp.zeros((n_seq, cc), F32)
            for j in range(conv_w):
                y = y + xin_ref[t + j] * cw_ref[j:j + 1, :]
            ys.append(y)
        y = jnp.concatenate(ys, axis=0)
        cs_ref[...] = xin_ref[n_t:n_t + pre_rows]
        ca_ref[...] = jax.nn.silu(_layer_norm(y + cb_ref[...], clg_ref[...], clb_ref[...])).astype(BF)
    else:
        halo = 32
        @pl.when(pl.program_id(1) == 0)
        def _():
            xin_ref[0:halo, :] = jnp.zeros((halo, cc), F32)
        xin_ref[halo:halo + tm, :] = glu
        rb = 64
        off = halo - pre_rows
        for r0 in range(0, tm, rb):
            y = jnp.zeros((rb, cc), F32)
            for j in range(conv_w):
                y = y + xin_ref[r0 + off + j:r0 + off + j + rb, :] * cw_ref[j:j + 1, :]
            y = jax.nn.silu(_layer_norm(y + cb_ref[...], clg_ref[...], clb_ref[...]))
            ca_ref[r0:r0 + rb, :] = y.astype(BF)
        cs_ref[...] = xin_ref[halo + tm - pre_rows:halo + tm, :]
        xin_ref[0:halo, :] = xin_ref[tm:tm + halo, :]

    uv = jax.nn.gelu(_dot(h, wuv_ref[...]))
    cg = uv.shape[1] // 2
    u = uv[:, :cg]
    vn = _layer_norm(uv[:, cg:], glg_ref[...], glb_ref[...])
    if sample:
        outs[11][...] = vn
    vb = vn.astype(BF)
    gd = cg // N_GMLP_GROUPS
    chunk = ws_ref.shape[-1]
    ri = lax.broadcasted_iota(jnp.int32, (chunk, chunk), 0)
    ci = lax.broadcasted_iota(jnp.int32, (chunk, chunk), 1)
    wsm = [jnp.where(ci <= ri, ws_ref[g], 0.0).astype(BF) for g in range(N_GMLP_GROUPS)]
    for n in range(tm // chunk):
        rs = slice(n * chunk, (n + 1) * chunk)
        mixed = jnp.concatenate(
            [_dot(wsm[g], vb[rs, g * gd:(g + 1) * gd]) for g in range(N_GMLP_GROUPS)], axis=1)
        gm_ref[rs, :] = (u[rs, :] * (mixed + bsb_ref[...])).astype(BF)


def _mix_in(x, layer, w, *, tm, n_seq, seq_tiles, sample, state=None):
    n, d = x.shape
    conv_w = w["conv_w"].shape[1]
    cc = w["conv_w"].shape[2]
    cg = w["gmlp_ln_g"].shape[2]
    pre_rows = conv_w - 1
    if sample:
        grid = (1,)
        row = lambda i: (0, 0)
        res = lambda shape: _resident(shape, layer)
        cs_shape = (pre_rows, n_seq, cc)
        cs_spec = pl.BlockSpec(cs_shape, lambda i: (0, 0, 0))
        xin = pltpu.VMEM((pre_rows + tm // n_seq, n_seq, cc), F32)
    else:
        grid = (n_seq, seq_tiles)
        row = lambda b, t: (b * seq_tiles + t, 0)
        res = lambda shape: _resident(shape, layer)
        cs_shape = (n_seq, pre_rows, cc)
        cs_spec = pl.BlockSpec((None, pre_rows, cc), lambda b, t: (b, 0, 0))
        xin = pltpu.VMEM((32 + tm, cc), F32)

    wnames = ["wa", "wq", "wkv", "wkd", "wvd", "wqi", "wkw", "wkk", "wuv", "wg"]
    in_specs = [pl.BlockSpec((tm, d), row), res((1, d))]
    in_specs += [res(w[k].shape[1:]) for k in wnames]
    in_specs += [res((conv_w, cc)), res((1, cc)), res((1, cc)), res((1, cc)),
                 res((1, cg)), res((1, cg)),
                 res(w["ws"].shape[1:]), res(w["bsb"].shape[1:])]
    args = [x, w["mix_pre"]] + [w[k] for k in wnames]
    args += [w["conv_w"], w["conv_b"], w["conv_ln_g"], w["conv_ln_b"], w["gmlp_ln_g"], w["gmlp_ln_b"],
             w["ws"], w["bsb"]]
    if sample:
        in_specs.append(pl.BlockSpec(cs_shape, lambda i: (0, 0, 0)))
        args.append(state)

    widths = [("q", 512, BF), ("kv", 256, F32), ("kd", 256, BF), ("vd", 256, BF), ("qi", 512, BF),
              ("kw", 128, F32), ("kk", 128, BF), ("ca", cc, BF), ("gm", cg, BF),
              ("gates", w["wg"].shape[2], BF)]
    out_shape = [jax.ShapeDtypeStruct((n, wd), dt) for _, wd, dt in widths]
    out_specs = [pl.BlockSpec((tm, wd), row) for _, wd, _ in widths]
    out_shape.append(jax.ShapeDtypeStruct(cs_shape, F32))
    out_specs.append(cs_spec)
    if sample:
        out_shape.append(jax.ShapeDtypeStruct((n, cg), F32))
        out_specs.append(pl.BlockSpec((tm, cg), row))

    outs = pl.pallas_call(
        functools.partial(_mix_in_body, tm=tm, conv_w=conv_w, sample=sample, n_seq=n_seq),
        out_shape=out_shape, grid=grid, in_specs=in_specs, out_specs=out_specs,
        scratch_shapes=[xin], compiler_params=_cparams(len(grid)),
        name="mix_in_sample" if sample else "mix_in",
    )(*args)
    names = [nm for nm, _, _ in widths] + ["conv_state"] + (["gmlp_v"] if sample else [])
    return dict(zip(names, outs))


def _ordered_bits_to_float(u):
    t = u ^ jnp.int32(-2 ** 31)
    fb = t ^ (lax.shift_right_arithmetic(t, 31) & jnp.int32(0x7FFFFFFF))
    return lax.bitcast_convert_type(fb, F32)


def _row_total(cnt, ones_bf):
    return _dot(cnt.astype(BF), ones_bf)


def _select_topk(sc_ref, nk, rows, width, k_sel):
    nh = width // LANES
    ones_bf = jnp.ones((LANES, LANES), BF)
    kf = float(k_sel)

    def halves(x):
        return [x[:, i * LANES:(i + 1) * LANES] for i in range(nh)]

    def count(pred):
        def body(c, cnt):
            for xh in halves(sc_ref[c]):
                cnt = cnt + jnp.where(pred(xh), 1.0, 0.0)
            return cnt
        return _row_total(lax.fori_loop(0, nk, body, jnp.zeros((rows, LANES), F32)), ones_bf)

    def bit_body(i, u):
        cand = u | jnp.left_shift(jnp.int32(1), 31 - i)
        thr = _ordered_bits_to_float(cand)
        return jnp.where(count(lambda xh: xh >= thr) >= kf, cand, u)

    u = lax.fori_loop(0, 32, bit_body, jnp.zeros((rows, LANES), jnp.int32))
    thr = _ordered_bits_to_float(u)
    thr = jnp.where(thr >= F32_LOWEST, thr, F32_LOWEST)

    n_ge = count(lambda xh: xh >= thr)

    @pl.when(jnp.max(n_ge) > kf)
    def _():
        need = kf - count(lambda xh: xh > thr)
        ri = lax.broadcasted_iota(jnp.int32, (LANES, LANES), 0)
        ci = lax.broadcasted_iota(jnp.int32, (LANES, LANES), 1)
        before = jnp.where(ri < ci, 1.0, 0.0).astype(BF)

        def fix(c, seen):
            out = []
            for xh in halves(sc_ref[c]):
                tie = xh == thr
                tf = jnp.where(tie, 1.0, 0.0).astype(BF)
                rank = seen + _dot(tf, before)
                out.append(jnp.where(jnp.where(tie, rank, -1.0) >= need, -jnp.inf, xh))
                seen = seen + _dot(tf, ones_bf)
            sc_ref[c] = out[0] if nh == 1 else jnp.concatenate(out, axis=1)
            return seen

        lax.fori_loop(0, nk, fix, jnp.zeros((rows, LANES), F32))

    def to_mask(c, carry):
        x = sc_ref[c]
        thr_w = thr if nh == 1 else jnp.concatenate([thr] * nh, axis=1)
        sc_ref[c] = jnp.where(x >= thr_w, 0.0, MASKED)
        return carry

    lax.fori_loop(0, nk, to_mask, 0)


def _online_softmax_step(s, v, m, l, acc):
    m_new = jnp.maximum(m, jnp.max(s, axis=1, keepdims=True))
    alpha = jnp.exp2(m - m_new)
    p = jnp.exp2(s - m_new)
    l = alpha * l + jnp.sum(p, axis=1, keepdims=True)
    acc = alpha * acc + _dot(p.astype(BF), v)
    return m_new, l, acc


def _attn_prompt_body(q_ref, qi_ref, kw_ref, kk_ref, kd_ref, vd_ref, band_ref, o_ref,
                      sc_ref, wb_ref, *, tq, k_sel):
    j = pl.program_id(1)
    nk = j + 1
    lane = lax.broadcasted_iota(jnp.int32, (1, LANES), 1)
    half_mask = [jnp.where(lane < IDX_DIM, 1.0, 0.0).astype(BF),
                 jnp.where(lane >= IDX_DIM, 1.0, 0.0).astype(BF)]

    kw = kw_ref[...]
    for h in range(N_IDX_HEADS):
        wb_ref[h] = jnp.broadcast_to(kw[:, IDX_DIM + h:IDX_DIM + h + 1], (tq, LANES))
    qi = qi_ref[...]
    qih = [qi[:, (h // 2) * LANES:(h // 2 + 1) * LANES] * half_mask[h % 2] for h in range(N_IDX_HEADS)]
    q_pos = lax.broadcasted_iota(jnp.int32, (tq, tq), 0) + j * tq

    def score_chunk(c, carry):
        k0 = pl.multiple_of(c * tq, tq)
        kic = kk_ref[pl.ds(k0, tq), :]
        acc = jnp.zeros((tq, tq), F32)
        for h in range(N_IDX_HEADS):
            s = lax.dot_general(qih[h], kic, NT_DIMS, preferred_element_type=F32)
            wgt = jnp.concatenate([wb_ref[h]] * (tq // LANES), axis=1)
            acc = acc + jnp.maximum(s, 0.0) * wgt
        k_pos = lax.broadcasted_iota(jnp.int32, (tq, tq), 1) + k0
        sc_ref[c] = jnp.where(k_pos <= q_pos, acc, -jnp.inf)
        return carry

    lax.fori_loop(0, nk, score_chunk, 0)
    _select_topk(sc_ref, nk, tq, tq, k_sel)

    q = q_ref[...]
    lane_f = lax.broadcasted_iota(jnp.int32, (tq, LANES), 1)
    j_prev = jnp.maximum(j - 1, 0)
    prev_mask = jnp.where(j > 0, 0.0, MASKED)
    o_heads = []
    for h in range(N_HEADS):
        g = h // GROUP_SIZE
        gs = slice(g * LANES, (g + 1) * LANES)
        qh = q[:, (h // 2) * LANES:(h // 2 + 1) * LANES] * half_mask[h % 2]

        def logits(c):
            k0 = pl.multiple_of(c * tq, tq)
            s = lax.dot_general(qh, kd_ref[pl.ds(k0, tq), gs], NT_DIMS, preferred_element_type=F32)
            return s + sc_ref[c], vd_ref[pl.ds(k0, tq), gs]

        def far(c, carry):
            s, v = logits(c)
            return _online_softmax_step(s, v, *carry)

        carry = (jnp.full((tq, 1), M_INIT, F32), jnp.zeros((tq, 1), F32), jnp.zeros((tq, LANES), F32))
        carry = lax.fori_loop(0, j_prev, far, carry)
        s, v = logits(j_prev)
        carry = _online_softmax_step(s + band_ref[h, :, 0:tq] + prev_mask, v, *carry)
        s, v = logits(j)
        m, l, acc = _online_softmax_step(s + band_ref[h, :, tq:2 * tq], v, *carry)
        o_heads.append(acc / l)
    for p in range(N_HEADS // 2):
        o_ref[:, p * LANES:(p + 1) * LANES] = jnp.where(
            lane_f < HEAD_DIM, o_heads[2 * p], o_heads[2 * p + 1]).astype(BF)


def _attn_prompt(mi, band, *, n_seq, seq, tq, k_sel):
    nt = seq // tq
    row = lambda b, t: (b * nt + t, 0)
    per_seq = lambda b, t: (b, 0)
    return pl.pallas_call(
        functools.partial(_attn_prompt_body, tq=tq, k_sel=k_sel),
        out_shape=jax.ShapeDtypeStruct((n_seq * seq, N_HEADS * HEAD_DIM), BF),
        grid=(n_seq, nt),
        in_specs=[pl.BlockSpec((tq, 512), row), pl.BlockSpec((tq, 512), row), pl.BlockSpec((tq, 128), row),
                  pl.BlockSpec((seq, 128), per_seq), pl.BlockSpec((seq, 256), per_seq),
                  pl.BlockSpec((seq, 256), per_seq), _resident(band.shape)],
        out_specs=pl.BlockSpec((tq, N_HEADS * HEAD_DIM), row),
        scratch_shapes=[pltpu.VMEM((nt, tq, tq), F32), pltpu.VMEM((N_IDX_HEADS, tq, LANES), F32)],
        compiler_params=_cparams(2),
        name="attn_prompt",
    )(mi["q"], mi["qi"], mi["kw"], mi["kk"], mi["kd"], mi["vd"], band)


def _page_stream(cache_ref, buf_ref, sem_ref, pt_ref, layer, n_pages, page):
    def copy(seq, slot, p):
        return pltpu.make_async_copy(cache_ref.at[layer, pt_ref[seq, p]],
                                     buf_ref.at[slot, :, p * page:(p + 1) * page], sem_ref.at[slot])

    def start(seq, slot):
        for p in range(n_pages):
            copy(seq, slot, p).start()

    def wait(seq, slot):
        for p in range(n_pages):
            copy(seq, slot, p).wait()

    return start, wait


def _double_buffered(streams):
    i = pl.program_id(0)
    slot = i % 2

    @pl.when(i == 0)
    def _():
        for start, _ in streams:
            start(0, 0)

    @pl.when(i + 1 < pl.num_programs(0))
    def _():
        for start, _ in streams:
            start(i + 1, 1 - slot)

    for _, wait in streams:
        wait(i, slot)
    return slot


def _idx_scores_body(pt_ref, qi_ref, wq_ref, kin_ref, cki_ref, o_ref, buf_ref, sem_ref,
                     *, layer, n_pages, page, chunk):
    slot = _double_buffered([_page_stream(cki_ref, buf_ref, sem_ref, pt_ref, layer, n_pages, page)])
    qi = qi_ref[...]
    wq = wq_ref[...]
    rows_t = o_ref.shape[0]
    past = n_pages * page

    def head_sum(s):
        w = s.shape[1]
        wgt = wq if w == LANES else jnp.concatenate([wq] * (w // LANES), axis=1)
        s = (jnp.maximum(s, 0.0) * wgt).reshape(N_IDX_HEADS, rows_t, w)
        acc = s[0]
        for h in range(1, N_IDX_HEADS):
            acc = acc + s[h]
        return acc

    for c in range(past // chunk):
        cs = slice(c * chunk, (c + 1) * chunk)
        o_ref[:, cs] = head_sum(_dot(qi, buf_ref[slot, :, cs].astype(BF)))
    t = lax.broadcasted_iota(jnp.int32, (rows_t, page), 0)
    cc = lax.broadcasted_iota(jnp.int32, (rows_t, page), 1)
    o_ref[:, past:past + page] = jnp.where(cc <= t, head_sum(_dot(qi, kin_ref[...])), -jnp.inf)


def _select_body(sc_ref, o_ref, *, k_sel):
    o_ref[...] = sc_ref[...]
    _select_topk(o_ref, 1, o_ref.shape[1], o_ref.shape[2], k_sel)


def _attend_sample_body(pt_ref, q_ref, mask_ref, band_ref, kn_ref, vn_ref, ck_ref, cv_ref, o_ref,
                        kbuf_ref, vbuf_ref, s_ref, ksem_ref, vsem_ref, *, layer, n_pages, page, chunk):
    slot = _double_buffered([_page_stream(ck_ref, kbuf_ref, ksem_ref, pt_ref, layer, n_pages, page),
                             _page_stream(cv_ref, vbuf_ref, vsem_ref, pt_ref, layer, n_pages, page)])
    q = q_ref[...]
    rows = q.shape[0]
    rows_t = mask_ref.shape[0]
    past = n_pages * page
    n_chunks = past // chunk

    def add_mask(s, m):
        w = s.shape[1]
        return (s.reshape(N_HEADS, rows_t, w) + m[None]).reshape(rows, w)

    def lane_fold(x, fn, init):
        for i in range(x.shape[1] // LANES):
            init = fn(init, x[:, i * LANES:(i + 1) * LANES])
        return init

    mx = jnp.full((rows, LANES), M_INIT, F32)
    for c in range(n_chunks):
        cs = slice(c * chunk, (c + 1) * chunk)
        s = add_mask(_dot(q, kbuf_ref[slot, :, cs].astype(BF)), mask_ref[:, cs])
        if c == n_chunks - 1:
            s = jnp.concatenate([s[:, :chunk - page], s[:, chunk - page:] + band_ref[:, 0:page]], axis=1)
        s_ref[:, cs] = s
        mx = lane_fold(s, jnp.maximum, mx)
    s_new = add_mask(_dot(q, kn_ref[...]), mask_ref[:, past:past + page]) + band_ref[:, page:2 * page]
    m = jnp.max(jnp.maximum(mx, s_new), axis=1, keepdims=True)

    p_new = jnp.exp2(s_new - m)
    l = p_new
    acc = lax.dot_general(p_new.astype(BF), vn_ref[...], NT_DIMS, preferred_element_type=F32)
    for c in range(n_chunks):
        cs = slice(c * chunk, (c + 1) * chunk)
        p = jnp.exp2(s_ref[:, cs] - m)
        l = lane_fold(p, jnp.add, l)
        acc = acc + lax.dot_general(p.astype(BF), vbuf_ref[slot, :, cs].astype(BF), NT_DIMS,
                                    preferred_element_type=F32)
    o_ref[...] = acc / jnp.sum(l, axis=1, keepdims=True)


def _attn_sample(page_table, qi_rows, wq_rows, q_rows, ki_new_t, k_new_t, v_new_t, band, cki_t, ck_t, cv_t,
                 *, layer, k_sel):
    n_seq, n_pages = page_table.shape
    page = ck_t.shape[3]
    rows = q_rows.shape[1]
    rows_t = rows // N_HEADS
    past = n_pages * page
    width = past + page
    chunk = 1024 if past % 1024 == 0 else page
    per_seq3 = lambda b, pt: (b, 0, 0)
    any_spec = pl.BlockSpec(memory_space=pl.ANY)

    scores = pl.pallas_call(
        functools.partial(_idx_scores_body, layer=layer, n_pages=n_pages, page=page, chunk=chunk),
        out_shape=jax.ShapeDtypeStruct((n_seq, rows_t, width), F32),
        grid_spec=pltpu.PrefetchScalarGridSpec(
            num_scalar_prefetch=1, grid=(n_seq,),
            in_specs=[pl.BlockSpec((None, rows, IDX_DIM), per_seq3),
                      pl.BlockSpec((None, rows, LANES), per_seq3),
                      pl.BlockSpec((None, IDX_DIM, page), per_seq3),
                      any_spec],
            out_specs=pl.BlockSpec((None, rows_t, width), per_seq3),
            scratch_shapes=[pltpu.VMEM((2, IDX_DIM, past), F32), pltpu.SemaphoreType.DMA((2,))]),
        compiler_params=_cparams(1),
        name="idx_scores_sample",
    )(page_table, qi_rows, wq_rows, ki_new_t, cki_t)

    sel_rows = 64 if (n_seq * rows_t) % 64 == 0 else n_seq * rows_t
    mask = pl.pallas_call(
        functools.partial(_select_body, k_sel=k_sel),
        out_shape=jax.ShapeDtypeStruct((1, n_seq * rows_t, width), F32),
        grid=(n_seq * rows_t // sel_rows,),
        in_specs=[pl.BlockSpec((1, sel_rows, width), lambda i: (0, i, 0))],
        out_specs=pl.BlockSpec((1, sel_rows, width), lambda i: (0, i, 0)),
        compiler_params=_cparams(1),
        name="select_sample",
    )(scores.reshape(1, n_seq * rows_t, width)).reshape(n_seq, rows_t, width)

    return pl.pallas_call(
        functools.partial(_attend_sample_body, layer=layer, n_pages=n_pages, page=page, chunk=chunk),
        out_shape=jax.ShapeDtypeStruct((n_seq, rows, LANES), F32),
        grid_spec=pltpu.PrefetchScalarGridSpec(
            num_scalar_prefetch=1, grid=(n_seq,),
            in_specs=[pl.BlockSpec((None, rows, LANES), per_seq3),
                      pl.BlockSpec((None, rows_t, width), per_seq3),
                      pl.BlockSpec(band.shape, lambda b, pt: (0, 0), pipeline_mode=pl.Buffered(1)),
                      pl.BlockSpec((None, LANES, page), per_seq3),
                      pl.BlockSpec((None, LANES, page), per_seq3),
                      any_spec, any_spec],
            out_specs=pl.BlockSpec((None, rows, LANES), per_seq3),
            scratch_shapes=[pltpu.VMEM((2, LANES, past), F32), pltpu.VMEM((2, LANES, past), F32),
                            pltpu.VMEM((rows, past), F32),
                            pltpu.SemaphoreType.DMA((2,)), pltpu.SemaphoreType.DMA((2,))]),
        compiler_params=_cparams(1),
        name="attend_sample",
    )(page_table, q_rows, mask, band, k_new_t, v_new_t, ck_t, cv_t)


def _merge_body(x_ref, ca_ref, att_ref, gm_ref, gt_ref, wco_ref, wao_ref, wgo_ref, wout_ref, post_ref, o_ref):
    d = x_ref.shape[1]
    gt = gt_ref[...].astype(F32)
    m = (gt[:, 0:d] * _dot(ca_ref[...], wco_ref[...])
         + gt[:, d:2 * d] * _dot(att_ref[...], wao_ref[...])
         + gt[:, 2 * d:3 * d] * _dot(gm_ref[...], wgo_ref[...]))
    o_ref[...] = x_ref[...] + _rms(_dot(m.astype(BF), wout_ref[...]), post_ref[...])


def _merge(x, ca, att, gm, gates, layer, w, *, tm):
    n, d = x.shape
    row = lambda i: (i, 0)
    return pl.pallas_call(
        _merge_body,
        out_shape=jax.ShapeDtypeStruct((n, d), F32),
        grid=(n // tm,),
        in_specs=[pl.BlockSpec((tm, d), row), pl.BlockSpec((tm, ca.shape[1]), row),
                  pl.BlockSpec((tm, att.shape[1]), row), pl.BlockSpec((tm, gm.shape[1]), row),
                  pl.BlockSpec((tm, gates.shape[1]), row),
                  _resident(w["wco"].shape[1:], layer), _resident(w["wao"].shape[1:], layer),
                  _resident(w["wgo"].shape[1:], layer), _resident(w["wout"].shape[1:], layer),
                  _resident((1, d), layer)],
        out_specs=pl.BlockSpec((tm, d), row),
        compiler_params=_cparams(1),
        name="merge",
    )(x, ca, att, gm, gates, w["wco"], w["wao"], w["wgo"], w["wout"], w["mix_post"])


def _band_body(rb_ref, bucket_ref, o_ref):
    h = pl.program_id(0)
    bucket = bucket_ref[...]
    far = rb_ref[N_BUCKETS - 1, h]
    tile = jnp.zeros(bucket.shape, F32)
    for bkt in range(N_BUCKETS):
        tile = jnp.where(bucket == bkt, rb_ref[bkt, h] - far, tile)
    o_ref[...] = tile * LOG2E


def _band_bias(rel_bias, dist):
    max_exact = N_BUCKETS // 2
    d = jnp.maximum(dist, 1).astype(F32)
    large = max_exact + (jnp.log(d / max_exact) / math.log(MAX_DISTANCE / max_exact)
                         * (N_BUCKETS - max_exact)).astype(jnp.int32)
    bucket = jnp.where(dist < max_exact, dist, jnp.minimum(large, N_BUCKETS - 1))
    r, c = dist.shape
    return pl.pallas_call(
        _band_body,
        out_shape=jax.ShapeDtypeStruct((N_HEADS, r, c), F32),
        grid=(N_HEADS,),
        in_specs=[pl.BlockSpec(memory_space=pltpu.SMEM), pl.BlockSpec((r, c), lambda h: (0, 0))],
        out_specs=pl.BlockSpec((None, r, c), lambda h: (h, 0, 0)),
        compiler_params=_cparams(1),
        name="band_bias",
    )(rel_bias, bucket)


def _prep_weights(p):
    d = p["w_in"].shape[1]
    cc = p["conv_w"].shape[2]
    cg = p["gmlp_ln_g"].shape[1]
    hq = N_HEADS * HEAD_DIM
    hk = N_KV_HEADS * HEAD_DIM
    hi = N_IDX_HEADS * IDX_DIM
    win = p["w_in"].astype(BF)
    o = 0
    cols = {}
    for name, wd in (("a", 2 * cc), ("q", hq), ("k", hk), ("v", hk), ("qi", hi), ("ki", IDX_DIM),
                     ("wi", N_IDX_HEADS), ("uv", 2 * cg), ("g", 3 * d)):
        cols[name] = win[:, :, o:o + wd]
        o += wd
    k0, k1 = cols["k"][..., :HEAD_DIM], cols["k"][..., HEAD_DIM:]
    v0, v1 = cols["v"][..., :HEAD_DIM], cols["v"][..., HEAD_DIM:]
    pad = jnp.zeros(cols["wi"].shape[:2] + (LANES - IDX_DIM - N_IDX_HEADS,), BF)
    vec = lambda a: a[:, None, :]
    return {
        "wa": cols["a"], "wq": cols["q"],
        "wkv": jnp.concatenate([cols["k"], cols["v"]], axis=-1),
        "wkd": jnp.concatenate([k0, k0, k1, k1], axis=-1),
        "wvd": jnp.concatenate([v0, v0, v1, v1], axis=-1),
        "wqi": cols["qi"],
        "wkw": jnp.concatenate([cols["ki"], cols["wi"], pad], axis=-1),
        "wkk": jnp.concatenate([cols["ki"], cols["ki"]], axis=-1),
        "wuv": cols["uv"], "wg": cols["g"],
        "mix_pre": vec(p["mix_norm_pre"]), "mix_post": vec(p["mix_norm_post"]),
        "conv_w": p["conv_w"], "conv_b": vec(p["conv_b"]),
        "conv_ln_g": vec(p["conv_ln_g"]), "conv_ln_b": vec(p["conv_ln_b"]),
        "gmlp_ln_g": vec(p["gmlp_ln_g"]), "gmlp_ln_b": vec(p["gmlp_ln_b"]),
        "wco": p["w_conv_out"].astype(BF), "wao": p["w_attn_out"].astype(BF),
        "wgo": p["w_gmlp_out"].astype(BF), "wout": p["w_out"].astype(BF),
        "f1": (vec(p["ffn1_norm_pre"]), vec(p["ffn1_norm_post"]), p["ffn1_w_gate"].astype(BF),
               p["ffn1_w_up"].astype(BF), p["ffn1_w_down"].astype(BF)),
        "f2": (vec(p["ffn2_norm_pre"]), vec(p["ffn2_norm_post"]), p["ffn2_w_gate"].astype(BF),
               p["ffn2_w_up"].astype(BF), p["ffn2_w_down"].astype(BF)),
    }


def _gmlp_spatial(ws, bs, rows, n_seq):
    gd = LANES
    if n_seq is None:
        w = ws
        b = jnp.swapaxes(bs, 1, 2)
    else:
        t = rows // n_seq
        eye = jnp.eye(n_seq, dtype=ws.dtype)
        w = jnp.einsum("lgts,bc->lgtbsc", ws[:, :, :t, :t], eye).reshape(ws.shape[0], ws.shape[1], rows, rows)
        b = jnp.repeat(jnp.swapaxes(bs[:, :, :t], 1, 2), n_seq, axis=1)
    return w, jnp.repeat(b, gd, axis=2)


def kernel(x_prompt, x_sample, cache_k, cache_v, cache_idx_k, state_conv, page_table,
           ffn1_norm_pre, ffn1_norm_post, ffn1_w_gate, ffn1_w_up, ffn1_w_down,
           mix_norm_pre, mix_norm_post, w_in, conv_w, conv_b, conv_ln_g, conv_ln_b,
           w_conv_out, w_attn_out, rel_bias, gmlp_ln_g, gmlp_ln_b, gmlp_ws, gmlp_bs,
           w_gmlp_out, w_out, ffn2_norm_pre, ffn2_norm_post, ffn2_w_gate, ffn2_w_up, ffn2_w_down):
    params = dict(ffn1_norm_pre=ffn1_norm_pre, ffn1_norm_post=ffn1_norm_post, ffn1_w_gate=ffn1_w_gate,
                  ffn1_w_up=ffn1_w_up, ffn1_w_down=ffn1_w_down, mix_norm_pre=mix_norm_pre,
                  mix_norm_post=mix_norm_post, w_in=w_in, conv_w=conv_w, conv_b=conv_b,
                  conv_ln_g=conv_ln_g, conv_ln_b=conv_ln_b, w_conv_out=w_conv_out, w_attn_out=w_attn_out,
                  gmlp_ln_g=gmlp_ln_g, gmlp_ln_b=gmlp_ln_b, w_gmlp_out=w_gmlp_out, w_out=w_out,
                  ffn2_norm_pre=ffn2_norm_pre, ffn2_norm_post=ffn2_norm_post, ffn2_w_gate=ffn2_w_gate,
                  ffn2_w_up=ffn2_w_up, ffn2_w_down=ffn2_w_down)
    depth = w_in.shape[0]
    nb, seq, d = x_prompt.shape
    db, t_new, _ = x_sample.shape
    page = cache_k.shape[2]
    n_pages = page_table.shape[1]
    past = n_pages * page
    n_s = db * t_new
    cc = conv_w.shape[2]
    cg = gmlp_ln_g.shape[1]

    w = _prep_weights(params)
    wp = dict(w)
    wp["ws"], wp["bsb"] = _gmlp_spatial(gmlp_ws, gmlp_bs, GMLP_CHUNK, None)
    wsm = dict(w)
    wsm["ws"], wsm["bsb"] = _gmlp_spatial(gmlp_ws, gmlp_bs, n_s, db)

    tq = 256
    tm_p = 512 if seq % 512 == 0 else 256
    k_sel_p = min(TOPK_MAX, seq // 4)
    k_sel_s = min(TOPK_MAX, (past + t_new) // 4)

    r = jnp.arange(tq, dtype=jnp.int32)[:, None]
    c = jnp.arange(2 * tq, dtype=jnp.int32)[None, :]
    band_p = _band_bias(rel_bias, jnp.maximum(tq + r - c, 0))
    r8 = jnp.minimum(jnp.arange(8, dtype=jnp.int32), t_new - 1)[:, None]
    c2 = jnp.arange(2 * page, dtype=jnp.int32)[None, :]
    band_s = _band_bias(rel_bias, jnp.maximum(r8 + page - c2, 0)).reshape(N_HEADS * 8, 2 * page)

    ck_t = jnp.transpose(cache_k, (0, 1, 3, 4, 2)).reshape(depth, -1, N_KV_HEADS * HEAD_DIM, page)
    cv_t = jnp.transpose(cache_v, (0, 1, 3, 4, 2)).reshape(depth, -1, N_KV_HEADS * HEAD_DIM, page)
    cki_t = jnp.transpose(cache_idx_k, (0, 1, 3, 2))
    state_t = jnp.swapaxes(state_conv, 1, 2)

    xp = x_prompt.reshape(nb * seq, d)
    xs = jnp.swapaxes(x_sample, 0, 1).reshape(n_s, d)

    def heads_rows(a, width):
        a = a.reshape(t_new, db, -1, width).transpose(1, 2, 0, 3)
        a = jnp.pad(a, ((0, 0), (0, 0), (0, 8 - t_new), (0, 0)), mode="edge")
        return a.reshape(db, -1, width)

    outs = {k: [] for k in ("kp", "vp", "kip", "cp", "ks", "vs", "kis", "cs", "gv")}
    for l in range(depth):
        xp = _ffn(xp, l, *w["f1"], tm=tm_p)
        mi = _mix_in(xp, l, wp, tm=tm_p, n_seq=nb, seq_tiles=seq // tm_p, sample=False)
        att = _attn_prompt(mi, band_p, n_seq=nb, seq=seq, tq=tq, k_sel=k_sel_p)
        xp = _merge(xp, mi["ca"], att, mi["gm"], mi["gates"], l, w, tm=tm_p)
        xp = _ffn(xp, l, *w["f2"], tm=tm_p)
        kv = mi["kv"].reshape(nb, seq, 2, N_KV_HEADS, HEAD_DIM)
        outs["kp"].append(kv[:, :, 0])
        outs["vp"].append(kv[:, :, 1])
        outs["kip"].append(mi["kw"][:, :IDX_DIM].reshape(nb, seq, IDX_DIM))
        outs["cp"].append(mi["conv_state"])

        xs = _ffn(xs, l, *w["f1"], tm=n_s)
        ms = _mix_in(xs, l, wsm, tm=n_s, n_seq=db, seq_tiles=1, sample=True, state=state_t[l])
        kv_s = jnp.swapaxes(ms["kv"].reshape(t_new, db, 2, N_KV_HEADS * HEAD_DIM), 0, 1)
        ki_s = jnp.swapaxes(ms["kw"][:, :IDX_DIM].reshape(t_new, db, IDX_DIM), 0, 1)
        qi_rows = heads_rows(ms["qi"], IDX_DIM)
        wq_rows = heads_rows(jnp.broadcast_to(
            ms["kw"][:, IDX_DIM:IDX_DIM + N_IDX_HEADS, None], (n_s, N_IDX_HEADS, LANES)
        ).reshape(n_s, N_IDX_HEADS * LANES), LANES)
        qh = heads_rows(ms["q"], HEAD_DIM)
        zero = jnp.zeros_like(qh)
        first = (jnp.arange(N_HEADS * 8) < GROUP_SIZE * 8)[None, :, None]
        q_rows = jnp.concatenate([jnp.where(first, qh, zero), jnp.where(first, zero, qh)], axis=-1)
        new_t = lambda a: jnp.pad(jnp.swapaxes(a, 1, 2), ((0, 0), (0, 0), (0, page - t_new))).astype(BF)
        o = _attn_sample(page_table, qi_rows, wq_rows, q_rows, new_t(ki_s), new_t(kv_s[:, :, 0]),
                         new_t(kv_s[:, :, 1]), band_s, cki_t, ck_t, cv_t, layer=l, k_sel=k_sel_s)
        o = o.reshape(db, N_HEADS, 8, N_KV_HEADS, HEAD_DIM)[:, :, :t_new]
        o = jnp.concatenate([o[:, :GROUP_SIZE, :, 0], o[:, GROUP_SIZE:, :, 1]], axis=1)
        att_s = o.transpose(2, 0, 1, 3).reshape(n_s, N_HEADS * HEAD_DIM).astype(BF)
        xs = _merge(xs, ms["ca"], att_s, ms["gm"], ms["gates"], l, w, tm=n_s)
        xs = _ffn(xs, l, *w["f2"], tm=n_s)
        outs["ks"].append(kv_s[:, :, 0].reshape(db, t_new, N_KV_HEADS, HEAD_DIM))
        outs["vs"].append(kv_s[:, :, 1].reshape(db, t_new, N_KV_HEADS, HEAD_DIM))
        outs["kis"].append(ki_s)
        outs["cs"].append(jnp.swapaxes(ms["conv_state"], 0, 1))
        outs["gv"].append(jnp.swapaxes(ms["gmlp_v"].reshape(t_new, db, cg), 0, 1))

    yp = xp.reshape(nb, seq, d)
    ys = jnp.swapaxes(xs.reshape(t_new, db, d), 0, 1)
    st = lambda k: jnp.stack(outs[k])
    return (yp, ys, st("kp"), st("vp"), st("kip"), st("cp"),
            st("ks"), st("vs"), st("kis"), st("cs"), st("gv"))
```

```python
import functools
import math

import numpy as np
import jax
import jax.numpy as jnp
from jax import lax
from jax.experimental import pallas as pl
from jax.experimental.pallas import tpu as pltpu

F32 = jnp.float32
BF = jnp.bfloat16

N_HEADS = 8
N_KV_HEADS = 2
HEAD_DIM = 64
GROUP_SIZE = N_HEADS // N_KV_HEADS
N_IDX_HEADS = 8
IDX_DIM = 64
TOPK_MAX = 256
N_BUCKETS = 32
MAX_DISTANCE = 128
N_GMLP_GROUPS = 4
GMLP_CHUNK = 128
EPS = 1e-6

LANES = 128
MASKED = -1e30
M_INIT = -1e29
LOG2E = 1.4426950408889634
F32_LOWEST = -3.4028234663852886e38
VMEM_LIMIT = 56 * 1024 * 1024

NT_DIMS = (((1,), (1,)), ((), ()))


def _cparams(n_axes):
    return pltpu.CompilerParams(dimension_semantics=("arbitrary",) * n_axes,
                                vmem_limit_bytes=VMEM_LIMIT)


def _resident(shape, layer=None):
    nd = len(shape)
    if layer is None:
        return pl.BlockSpec(shape, lambda *_: (0,) * nd, pipeline_mode=pl.Buffered(1))
    return pl.BlockSpec((None,) + tuple(shape), lambda *_: (layer,) + (0,) * nd,
                        pipeline_mode=pl.Buffered(1))


def _rms(x, g):
    return x * lax.rsqrt(jnp.mean(x * x, axis=-1, keepdims=True) + EPS) * g


def _layer_norm(x, g, b):
    mu = jnp.mean(x, axis=-1, keepdims=True)
    xc = x - mu
    return xc * lax.rsqrt(jnp.mean(xc * xc, axis=-1, keepdims=True) + EPS) * g + b


def _dot(a, b):
    return jnp.dot(a, b, preferred_element_type=F32)


def _sigmoid(x):
    return 0.5 * jnp.tanh(0.5 * x) + 0.5


def _silu(x):
    return x * _sigmoid(x)


def _ffn_body(x_ref, pre_ref, post_ref, wg_ref, wu_ref, wd_ref, o_ref, *, ff_chunk):
    x = x_ref[...]
    h = _rms(x, pre_ref[...]).astype(BF)
    acc = jnp.zeros(x.shape, F32)
    for c in range(wg_ref.shape[1] // ff_chunk):
        sl = slice(c * ff_chunk, (c + 1) * ff_chunk)
        a = _silu(_dot(h, wg_ref[:, sl])) * _dot(h, wu_ref[:, sl])
        acc = acc + _dot(a.astype(BF), wd_ref[sl, :])
    o_ref[...] = x + 0.5 * _rms(acc, post_ref[...])


def _ffn(x, layer, pre, post, wg, wu, wd, *, tm):
    n, d = x.shape
    dff = wg.shape[2]
    return pl.pallas_call(
        functools.partial(_ffn_body, ff_chunk=256),
        out_shape=jax.ShapeDtypeStruct((n, d), F32),
        grid=(n // tm,),
        in_specs=[pl.BlockSpec((tm, d), lambda i: (i, 0)),
                  _resident((1, d), layer), _resident((1, d), layer),
                  _resident((d, dff), layer), _resident((d, dff), layer), _resident((dff, d), layer)],
        out_specs=pl.BlockSpec((tm, d), lambda i: (i, 0)),
        compiler_params=_cparams(1),
        name="ffn",
    )(x, pre, post, wg, wu, wd)


_MIX_IN = ["x", "pre", "wa", "wrow", "wuv", "wg", "cw", "cb", "clg", "clb", "glg", "glb", "ws", "bsb"]
_Q_SCALE = HEAD_DIM ** -0.5 * LOG2E
_HQ = N_HEADS * HEAD_DIM
_HI = N_IDX_HEADS * IDX_DIM
_HKV = N_KV_HEADS * HEAD_DIM


def _mix_in_body(*refs, names, tm, conv_w, sample, n_seq):
    r = dict(zip(names, refs))
    ca_ref, gm_ref, cs_ref, xin_ref = r["ca"], r["gm"], r["conv_state"], r["xin"]
    cw_ref, cb_ref, clg_ref, clb_ref = r["cw"], r["cb"], r["clg"], r["clb"]
    glg_ref, glb_ref, ws_ref, bsb_ref = r["glg"], r["glb"], r["ws"], r["bsb"]

    h = _rms(r["x"][...], r["pre"][...]).astype(BF)

    row = _dot(h, r["wrow"][...])
    if sample:
        r["q"][...] = (row[:, 0:_HQ] * _Q_SCALE).astype(BF)
        r["qi"][...] = row[:, _HQ:_HQ + _HI].astype(BF)
        o = _HQ + _HI
    else:
        col = lax.dot_general(r["wcol"][...], h, NT_DIMS, preferred_element_type=F32)
        r["qt"][...] = (col[0:_HQ] * _Q_SCALE).astype(BF)
        r["qit"][...] = col[_HQ:_HQ + _HI].astype(BF)
        vt = col[_HQ + _HI:_HQ + _HI + _HKV].astype(BF)
        tq = r["vt"].shape[2]
        for c in range(tm // tq):
            r["vt"][c] = vt[:, c * tq:(c + 1) * tq]
        r["wit"][...] = col[_HQ + _HI + _HKV:_HQ + _HI + _HKV + N_IDX_HEADS]
        r["ki"][...] = row[:, 2 * _HKV:2 * _HKV + IDX_DIM].astype(BF)
        for g in range(N_KV_HEADS):
            r["kg"][g] = row[:, g * HEAD_DIM:(g + 1) * HEAD_DIM].astype(BF)
        o = 0
    r["kv"][...] = row[:, o:o + 2 * _HKV]
    r["kw"][...] = row[:, o + 2 * _HKV:o + 2 * _HKV + LANES]
    r["gates"][...] = _sigmoid(_dot(h, r["wg"][...])).astype(BF)
    wa_ref, wuv_ref = r["wa"], r["wuv"]

    a = _dot(h, wa_ref[...])
    cc = a.shape[1] // 2
    glu = a[:, :cc] * _sigmoid(a[:, cc:])
    pre_rows = conv_w - 1
    if sample:
        n_t = tm // n_seq
        xin_ref[0:pre_rows] = r["state"][...]
        for t in range(n_t):
            xin_ref[pre_rows + t] = glu[t * n_seq:(t + 1) * n_seq, :]
        ys = []
        for t in range(n_t):
            y = jnp.zeros((n_seq, cc), F32)
            for j in range(conv_w):
                y = y + xin_ref[t + j] * cw_ref[j:j + 1, :]
            ys.append(y)
        y = jnp.concatenate(ys, axis=0)
        cs_ref[...] = xin_ref[n_t:n_t + pre_rows]
        ca_ref[...] = _silu(_layer_norm(y + cb_ref[...], clg_ref[...], clb_ref[...])).astype(BF)
    else:
        halo = 32
        @pl.when(pl.program_id(1) == 0)
        def _():
            xin_ref[0:halo, :] = jnp.zeros((halo, cc), F32)
        xin_ref[halo:halo + tm, :] = glu
        rb = 64
        off = halo - pre_rows
        sh_ref = r["shifted"]
        for b in range(8):
            n_rows = tm + 8 * (len(range(b, conv_w, 8)) - 1)
            sh_ref[b, 0:n_rows, :] = xin_ref[off + b:off + b + n_rows, :]
        for r0 in range(0, tm, rb):
            y = jnp.zeros((rb, cc), F32)
            for j in range(conv_w):
                y = y + sh_ref[j % 8, r0 + j - j % 8:r0 + j - j % 8 + rb, :] * cw_ref[j:j + 1, :]
            y = _silu(_layer_norm(y + cb_ref[...], clg_ref[...], clb_ref[...]))
            ca_ref[r0:r0 + rb, :] = y.astype(BF)
        cs_ref[...] = xin_ref[halo + tm - pre_rows:halo + tm, :]
        xin_ref[0:halo, :] = xin_ref[tm:tm + halo, :]

    uv = jax.nn.gelu(_dot(h, wuv_ref[...]))
    cg = uv.shape[1] // 2
    u = uv[:, :cg]
    vn = _layer_norm(uv[:, cg:], glg_ref[...], glb_ref[...])
    if sample:
        r["gmlp_v"][...] = vn
    vb = vn.astype(BF)
    gd = cg // N_GMLP_GROUPS
    chunk = ws_ref.shape[-1]
    ri = lax.broadcasted_iota(jnp.int32, (chunk, chunk), 0)
    ci = lax.broadcasted_iota(jnp.int32, (chunk, chunk), 1)
    wsm = [jnp.where(ci <= ri, ws_ref[g], 0.0).astype(BF) for g in range(N_GMLP_GROUPS)]
    for n in range(tm // chunk):
        rs = slice(n * chunk, (n + 1) * chunk)
        mixed = jnp.concatenate(
            [_dot(wsm[g], vb[rs, g * gd:(g + 1) * gd]) for g in range(N_GMLP_GROUPS)], axis=1)
        gm_ref[rs, :] = (u[rs, :] * (mixed + bsb_ref[...])).astype(BF)


def _mix_in(x, layer, w, *, tm, n_seq, seq_tiles, sample, state=None):
    n, d = x.shape
    conv_w = w["conv_w"].shape[1]
    cc = w["conv_w"].shape[2]
    cg = w["gmlp_ln_g"].shape[2]
    pre_rows = conv_w - 1
    if sample:
        grid = (1,)
        row = lambda i: (0, 0)
        res = lambda shape: _resident(shape, layer)
        cs_shape = (pre_rows, n_seq, cc)
        cs_spec = pl.BlockSpec(cs_shape, lambda i: (0, 0, 0))
        scratch = {"xin": pltpu.VMEM((pre_rows + tm // n_seq, n_seq, cc), F32)}
    else:
        grid = (n_seq, seq_tiles)
        row = lambda b, t: (b * seq_tiles + t, 0)
        res = lambda shape: _resident(shape, layer)
        cs_shape = (n_seq, pre_rows, cc)
        cs_spec = pl.BlockSpec((None, pre_rows, cc), lambda b, t: (b, 0, 0))
        scratch = {"xin": pltpu.VMEM((32 + tm, cc), F32),
                   "shifted": pltpu.VMEM((8, tm + 8 * ((conv_w - 1) // 8), cc), F32)}

    wrow = w["wrow_s"] if sample else w["wrow_p"]
    in_names = list(_MIX_IN)
    args = [x, w["mix_pre"], w["wa"], wrow, w["wuv"], w["wg"], w["conv_w"], w["conv_b"], w["conv_ln_g"],
            w["conv_ln_b"], w["gmlp_ln_g"], w["gmlp_ln_b"], w["ws"], w["bsb"]]
    in_specs = [pl.BlockSpec((tm, d), row)] + [res(a.shape[1:]) for a in args[1:]]
    if sample:
        in_names.append("state")
        in_specs.append(pl.BlockSpec(cs_shape, lambda i: (0, 0, 0)))
        args.append(state)
    else:
        in_names.append("wcol")
        in_specs.append(res(w["wcol"].shape[1:]))
        args.append(w["wcol"])

    rows = lambda wd, dt: ((n, wd), (tm, wd), row, dt)
    outs = {"kv": rows(2 * _HKV, F32), "kw": rows(LANES, F32), "ca": rows(cc, BF), "gm": rows(cg, BF),
            "gates": rows(w["wg"].shape[2], BF), "conv_state": (cs_shape, cs_spec.block_shape, cs_spec.index_map, F32)}
    if sample:
        outs.update(q=rows(_HQ, BF), qi=rows(_HI, BF), gmlp_v=rows(cg, F32))
    else:
        tq = 256
        cols = lambda ht, dt: ((ht, n), (ht, tm), lambda b, t: (0, b * seq_tiles + t), dt)
        outs.update(qt=cols(_HQ, BF), qit=cols(_HI, BF), wit=cols(N_IDX_HEADS, F32), ki=rows(IDX_DIM, BF),
                    kg=((N_KV_HEADS, n, HEAD_DIM), (N_KV_HEADS, tm, HEAD_DIM),
                        lambda b, t: (0, b * seq_tiles + t, 0), BF),
                    vt=((n // tq, _HKV, tq), (tm // tq, _HKV, tq), lambda b, t: (b * seq_tiles + t, 0, 0), BF))
    out_names = list(outs)

    res_out = pl.pallas_call(
        functools.partial(_mix_in_body, names=in_names + out_names + list(scratch), tm=tm, conv_w=conv_w,
                          sample=sample, n_seq=n_seq),
        out_shape=[jax.ShapeDtypeStruct(outs[k][0], outs[k][3]) for k in out_names],
        grid=grid, in_specs=in_specs,
        out_specs=[pl.BlockSpec(outs[k][1], outs[k][2]) for k in out_names],
        scratch_shapes=list(scratch.values()), compiler_params=_cparams(len(grid)),
        name="mix_in_sample" if sample else "mix_in",
    )(*args)
    return dict(zip(out_names, res_out))


def _ordered_bits_to_float(u):
    t = u ^ jnp.int32(-2 ** 31)
    fb = t ^ (lax.shift_right_arithmetic(t, 31) & jnp.int32(0x7FFFFFFF))
    return lax.bitcast_convert_type(fb, F32)


def _row_total(cnt, ones_bf):
    return _dot(cnt.astype(BF), ones_bf)


def _select_topk(sc_ref, nk, rows, width, k_sel):
    nh = width // LANES
    ones_bf = jnp.ones((LANES, LANES), BF)
    kf = float(k_sel)

    def halves(x):
        return [x[:, i * LANES:(i + 1) * LANES] for i in range(nh)]

    def count(pred):
        def body(c, cnt):
            for xh in halves(sc_ref[c]):
                cnt = cnt + jnp.where(pred(xh), 1.0, 0.0)
            return cnt
        return _row_total(lax.fori_loop(0, nk, body, jnp.zeros((rows, LANES), F32)), ones_bf)

    def bit_body(i, u):
        cand = u | jnp.left_shift(jnp.int32(1), 31 - i)
        thr = _ordered_bits_to_float(cand)
        return jnp.where(count(lambda xh: xh >= thr) >= kf, cand, u)

    u = lax.fori_loop(0, 32, bit_body, jnp.zeros((rows, LANES), jnp.int32))
    thr = _ordered_bits_to_float(u)
    thr = jnp.where(thr >= F32_LOWEST, thr, F32_LOWEST)

    n_ge = count(lambda xh: xh >= thr)

    @pl.when(jnp.max(n_ge) > kf)
    def _():
        need = kf - count(lambda xh: xh > thr)
        ri = lax.broadcasted_iota(jnp.int32, (LANES, LANES), 0)
        ci = lax.broadcasted_iota(jnp.int32, (LANES, LANES), 1)
        before = jnp.where(ri < ci, 1.0, 0.0).astype(BF)

        def fix(c, seen):
            out = []
            for xh in halves(sc_ref[c]):
                tie = xh == thr
                tf = jnp.where(tie, 1.0, 0.0).astype(BF)
                rank = seen + _dot(tf, before)
                out.append(jnp.where(jnp.where(tie, rank, -1.0) >= need, -jnp.inf, xh))
                seen = seen + _dot(tf, ones_bf)
            sc_ref[c] = out[0] if nh == 1 else jnp.concatenate(out, axis=1)
            return seen

        lax.fori_loop(0, nk, fix, jnp.zeros((rows, LANES), F32))

    def to_mask(c, carry):
        x = sc_ref[c]
        thr_w = thr if nh == 1 else jnp.concatenate([thr] * nh, axis=1)
        sc_ref[c] = jnp.where(x >= thr_w, 0.0, MASKED)
        return carry

    lax.fori_loop(0, nk, to_mask, 0)


def _select_topk_cols(sc_ref, nk, n_keys, n_q, k_sel):
    kf = float(k_sel)
    groups = n_keys // 8

    def count(pred):
        def body(c, cnt):
            x = sc_ref[c].reshape(groups, 8, n_q)
            return cnt + jnp.sum(jnp.where(pred(x), 1.0, 0.0), axis=0)
        cnt = lax.fori_loop(0, nk, body, jnp.zeros((8, n_q), F32))
        return jnp.sum(cnt, axis=0, keepdims=True)

    def rows8(v):
        return jnp.broadcast_to(v, (8, n_q))[None]

    def bit_body(i, u):
        cand = u | jnp.left_shift(jnp.int32(1), 31 - i)
        thr8 = rows8(_ordered_bits_to_float(cand))
        return jnp.where(count(lambda x: x >= thr8) >= kf, cand, u)

    u = lax.fori_loop(0, 32, bit_body, jnp.zeros((1, n_q), jnp.int32))
    thr = _ordered_bits_to_float(u)
    thr = jnp.where(thr >= F32_LOWEST, thr, F32_LOWEST)
    thr8 = rows8(thr)

    @pl.when(jnp.max(count(lambda x: x >= thr8)) > kf)
    def _():
        need = kf - count(lambda x: x > thr8)
        ri = lax.broadcasted_iota(jnp.int32, (n_keys, n_keys), 0)
        ci = lax.broadcasted_iota(jnp.int32, (n_keys, n_keys), 1)
        before = jnp.where(ci < ri, 1.0, 0.0).astype(BF)

        def fix(c, seen):
            x = sc_ref[c]
            tie = x == thr
            tf = jnp.where(tie, 1.0, 0.0)
            rank = seen + _dot(before, tf.astype(BF))
            sc_ref[c] = jnp.where(jnp.where(tie, rank, -1.0) >= need, -jnp.inf, x)
            return seen + jnp.sum(tf, axis=0, keepdims=True)

        lax.fori_loop(0, nk, fix, jnp.zeros((1, n_q), F32))

    def to_mask(c, carry):
        sc_ref[c] = jnp.where(sc_ref[c] >= thr, 0.0, MASKED)
        return carry

    lax.fori_loop(0, nk, to_mask, 0)


def _attn_prompt_body(qt_ref, qit_ref, wit_ref, ki_ref, kg_ref, vt_ref, band_ref, o_ref, sc_ref,
                      *, tq, k_sel, heads_per_stream):
    j = pl.program_id(1)
    nk = j + 1

    def heads_along_lanes(x, heads, width):
        return jnp.concatenate([x[h * width:(h + 1) * width, :] for h in heads], axis=1)

    qi_all = heads_along_lanes(qit_ref[...], range(N_IDX_HEADS), IDX_DIM)
    w_all = heads_along_lanes(wit_ref[...], range(N_IDX_HEADS), 1)
    halves = tq // LANES
    q_pos = lax.broadcasted_iota(jnp.int32, (tq, LANES), 1) + j * tq

    def score_chunk(c, carry):
        k0 = pl.multiple_of(c * tq, tq)
        kic = ki_ref[pl.ds(k0, tq), :]
        k_pos = lax.broadcasted_iota(jnp.int32, (tq, LANES), 0) + k0
        for i in range(halves):
            acc = None
            for h in range(N_IDX_HEADS):
                ls = slice(h * tq + i * LANES, h * tq + (i + 1) * LANES)
                term = jnp.maximum(_dot(kic, qi_all[:, ls]), 0.0) * w_all[:, ls]
                acc = term if acc is None else acc + term
            sc_ref[c, :, i * LANES:(i + 1) * LANES] = jnp.where(k_pos <= q_pos + i * LANES, acc, -jnp.inf)
        return carry

    lax.fori_loop(0, nk, score_chunk, 0)
    _select_topk_cols(sc_ref, nk, tq, tq, k_sel)

    qt = qt_ref[...]
    n_streams = N_HEADS // heads_per_stream
    per_group = GROUP_SIZE // heads_per_stream
    wide = heads_per_stream * tq
    q_s = [heads_along_lanes(qt, range(i * heads_per_stream, (i + 1) * heads_per_stream), HEAD_DIM)
           for i in range(n_streams)]

    def chunk_step(c, carry, bias):
        k0 = pl.multiple_of(c * tq, tq)
        mask = jnp.concatenate([sc_ref[c]] * heads_per_stream, axis=1)
        logits = []
        for i in range(n_streams):
            g = i // per_group
            s = _dot(kg_ref[g, pl.ds(k0, tq), :], q_s[i]) + mask
            if bias is not None:
                s = s + bias(g, slice((i % per_group) * wide, (i % per_group + 1) * wide))
            logits.append(s)
        stats = []
        for i in range(n_streams):
            m, l, _ = carry[i]
            m_new = jnp.maximum(m, jnp.max(logits[i], axis=0, keepdims=True))
            alpha = jnp.exp2(m - m_new)
            p = jnp.exp2(logits[i] - m_new)
            stats.append((m_new, alpha, alpha * l + jnp.sum(p, axis=0, keepdims=True), p.astype(BF)))
        out = []
        for i in range(n_streams):
            g = i // per_group
            m_new, alpha, l_new, p = stats[i]
            vt = vt_ref[c, g * HEAD_DIM:(g + 1) * HEAD_DIM, :]
            out.append((m_new, l_new, alpha * carry[i][2] + _dot(vt, p)))
        return out

    carry = [(jnp.full((1, wide), M_INIT, F32), jnp.zeros((1, wide), F32), jnp.zeros((HEAD_DIM, wide), F32))
             for _ in range(n_streams)]
    j_prev = jnp.maximum(j - 1, 0)
    prev_mask = jnp.where(j > 0, 0.0, MASKED)
    carry = lax.fori_loop(0, j_prev, lambda c, carry: chunk_step(c, carry, None), carry)
    carry = chunk_step(j_prev, carry, lambda g, ls: band_ref[g, 0, :, ls] + prev_mask)
    carry = chunk_step(j, carry, lambda g, ls: band_ref[g, 1, :, ls])
    o_t = []
    for m, l, acc in carry:
        o_s = acc / l
        o_t += [o_s[:, hh * tq:(hh + 1) * tq] for hh in range(heads_per_stream)]
    o_ref[...] = jnp.concatenate(o_t, axis=0).T.astype(BF)


def _attn_prompt(mi, band, *, n_seq, seq, tq, k_sel):
    nt = seq // tq
    col = lambda b, t: (0, b * nt + t)
    hd = N_HEADS * HEAD_DIM
    return pl.pallas_call(
        functools.partial(_attn_prompt_body, tq=tq, k_sel=k_sel, heads_per_stream=2),
        out_shape=jax.ShapeDtypeStruct((n_seq * seq, hd), BF),
        grid=(n_seq, nt),
        in_specs=[pl.BlockSpec((hd, tq), col), pl.BlockSpec((N_IDX_HEADS * IDX_DIM, tq), col),
                  pl.BlockSpec((N_IDX_HEADS, tq), col),
                  pl.BlockSpec((seq, IDX_DIM), lambda b, t: (b, 0)),
                  pl.BlockSpec((N_KV_HEADS, seq, HEAD_DIM), lambda b, t: (0, b, 0)),
                  pl.BlockSpec((nt, N_KV_HEADS * HEAD_DIM, tq), lambda b, t: (b, 0, 0)),
                  _resident(band.shape)],
        out_specs=pl.BlockSpec((tq, hd), lambda b, t: (b * nt + t, 0)),
        scratch_shapes=[pltpu.VMEM((nt, tq, tq), F32)],
        compiler_params=_cparams(2),
        name="attn_prompt",
    )(mi["qt"], mi["qit"], mi["wit"], mi["ki"], mi["kg"], mi["vt"], band)


def _page_stream(cache_ref, buf_ref, sem_ref, pt_ref, layer, n_pages, page):
    def copy(seq, slot, p):
        return pltpu.make_async_copy(cache_ref.at[layer, pt_ref[seq, p]],
                                     buf_ref.at[slot, :, p * page:(p + 1) * page], sem_ref.at[slot])

    def start(seq, slot):
        for p in range(n_pages):
            copy(seq, slot, p).start()

    def wait(seq, slot):
        for p in range(n_pages):
            copy(seq, slot, p).wait()

    return start, wait


def _double_buffered(streams):
    i = pl.program_id(0)
    slot = i % 2

    @pl.when(i == 0)
    def _():
        for start, _ in streams:
            start(0, 0)

    @pl.when(i + 1 < pl.num_programs(0))
    def _():
        for start, _ in streams:
            start(i + 1, 1 - slot)

    for _, wait in streams:
        wait(i, slot)
    return slot


def _idx_scores_body(pt_ref, qi_ref, wq_ref, kin_ref, cki_ref, o_ref, buf_ref, sem_ref,
                     *, layer, n_pages, page, chunk):
    slot = _double_buffered([_page_stream(cki_ref, buf_ref, sem_ref, pt_ref, layer, n_pages, page)])
    qi = qi_ref[...]
    wq = wq_ref[...]
    rows_t = o_ref.shape[0]
    past = n_pages * page

    def head_sum(s):
        w = s.shape[1]
        wgt = wq if w == LANES else jnp.concatenate([wq] * (w // LANES), axis=1)
        s = (jnp.maximum(s, 0.0) * wgt).reshape(N_IDX_HEADS, rows_t, w)
        acc = s[0]
        for h in range(1, N_IDX_HEADS):
            acc = acc + s[h]
        return acc

    for c in range(past // chunk):
        cs = slice(c * chunk, (c + 1) * chunk)
        o_ref[:, cs] = head_sum(_dot(qi, buf_ref[slot, :, cs].astype(BF)))
    t = lax.broadcasted_iota(jnp.int32, (rows_t, page), 0)
    cc = lax.broadcasted_iota(jnp.int32, (rows_t, page), 1)
    o_ref[:, past:past + page] = jnp.where(cc <= t, head_sum(_dot(qi, kin_ref[...])), -jnp.inf)


def _select_body(sc_ref, o_ref, *, k_sel):
    o_ref[...] = sc_ref[...]
    _select_topk(o_ref, 1, o_ref.shape[1], o_ref.shape[2], k_sel)


def _attend_sample_body(pt_ref, q_ref, mask_ref, band_ref, kn_ref, vn_ref, ck_ref, cv_ref, o_ref,
                        kbuf_ref, vbuf_ref, s_ref, ksem_ref, vsem_ref, *, layer, n_pages, page, chunk):
    slot = _double_buffered([_page_stream(ck_ref, kbuf_ref, ksem_ref, pt_ref, layer, n_pages, page),
                             _page_stream(cv_ref, vbuf_ref, vsem_ref, pt_ref, layer, n_pages, page)])
    q = q_ref[...]
    rows = q.shape[0]
    rows_t = mask_ref.shape[0]
    past = n_pages * page
    n_chunks = past // chunk

    def add_mask(s, m):
        w = s.shape[1]
        return (s.reshape(N_HEADS, rows_t, w) + m[None]).reshape(rows, w)

    def lane_fold(x, fn, init):
        for i in range(x.shape[1] // LANES):
            init = fn(init, x[:, i * LANES:(i + 1) * LANES])
        return init

    mx = jnp.full((rows, LANES), M_INIT, F32)
    for c in range(n_chunks):
        cs = slice(c * chunk, (c + 1) * chunk)
        s = add_mask(_dot(q, kbuf_ref[slot, :, cs].astype(BF)), mask_ref[:, cs])
        if c == n_chunks - 1:
            s = jnp.concatenate([s[:, :chunk - page], s[:, chunk - page:] + band_ref[:, 0:page]], axis=1)
        s_ref[:, cs] = s
        mx = lane_fold(s, jnp.maximum, mx)
    s_new = add_mask(_dot(q, kn_ref[...]), mask_ref[:, past:past + page]) + band_ref[:, page:2 * page]
    m = jnp.max(jnp.maximum(mx, s_new), axis=1, keepdims=True)

    p_new = jnp.exp2(s_new - m)
    l = p_new
    acc = lax.dot_general(p_new.astype(BF), vn_ref[...], NT_DIMS, preferred_element_type=F32)
    for c in range(n_chunks):
        cs = slice(c * chunk, (c + 1) * chunk)
        p = jnp.exp2(s_ref[:, cs] - m)
        l = lane_fold(p, jnp.add, l)
        acc = acc + lax.dot_general(p.astype(BF), vbuf_ref[slot, :, cs].astype(BF), NT_DIMS,
                                    preferred_element_type=F32)
    o_ref[...] = acc / jnp.sum(l, axis=1, keepdims=True)


def _attn_sample(page_table, qi_rows, wq_rows, q_rows, ki_new_t, k_new_t, v_new_t, band, cki_t, ck_t, cv_t,
                 *, layer, k_sel):
    n_seq, n_pages = page_table.shape
    page = ck_t.shape[3]
    rows = q_rows.shape[1]
    rows_t = rows // N_HEADS
    past = n_pages * page
    width = past + page
    chunk = 1024 if past % 1024 == 0 else page
    per_seq3 = lambda b, pt: (b, 0, 0)
    any_spec = pl.BlockSpec(memory_space=pl.ANY)

    scores = pl.pallas_call(
        functools.partial(_idx_scores_body, layer=layer, n_pages=n_pages, page=page, chunk=chunk),
        out_shape=jax.ShapeDtypeStruct((n_seq, rows_t, width), F32),
        grid_spec=pltpu.PrefetchScalarGridSpec(
            num_scalar_prefetch=1, grid=(n_seq,),
            in_specs=[pl.BlockSpec((None, rows, IDX_DIM), per_seq3),
                      pl.BlockSpec((None, rows, LANES), per_seq3),
                      pl.BlockSpec((None, IDX_DIM, page), per_seq3),
                      any_spec],
            out_specs=pl.BlockSpec((None, rows_t, width), per_seq3),
            scratch_shapes=[pltpu.VMEM((2, IDX_DIM, past), F32), pltpu.SemaphoreType.DMA((2,))]),
        compiler_params=_cparams(1),
        name="idx_scores_sample",
    )(page_table, qi_rows, wq_rows, ki_new_t, cki_t)

    sel_rows = 64 if (n_seq * rows_t) % 64 == 0 else n_seq * rows_t
    mask = pl.pallas_call(
        functools.partial(_select_body, k_sel=k_sel),
        out_shape=jax.ShapeDtypeStruct((1, n_seq * rows_t, width), F32),
        grid=(n_seq * rows_t // sel_rows,),
        in_specs=[pl.BlockSpec((1, sel_rows, width), lambda i: (0, i, 0))],
        out_specs=pl.BlockSpec((1, sel_rows, width), lambda i: (0, i, 0)),
        compiler_params=_cparams(1),
        name="select_sample",
    )(scores.reshape(1, n_seq * rows_t, width)).reshape(n_seq, rows_t, width)

    return pl.pallas_call(
        functools.partial(_attend_sample_body, layer=layer, n_pages=n_pages, page=page, chunk=chunk),
        out_shape=jax.ShapeDtypeStruct((n_seq, rows, LANES), F32),
        grid_spec=pltpu.PrefetchScalarGridSpec(
            num_scalar_prefetch=1, grid=(n_seq,),
            in_specs=[pl.BlockSpec((None, rows, LANES), per_seq3),
                      pl.BlockSpec((None, rows_t, width), per_seq3),
                      pl.BlockSpec(band.shape, lambda b, pt: (0, 0), pipeline_mode=pl.Buffered(1)),
                      pl.BlockSpec((None, LANES, page), per_seq3),
                      pl.BlockSpec((None, LANES, page), per_seq3),
                      any_spec, any_spec],
            out_specs=pl.BlockSpec((None, rows, LANES), per_seq3),
            scratch_shapes=[pltpu.VMEM((2, LANES, past), F32), pltpu.VMEM((2, LANES, past), F32),
                            pltpu.VMEM((rows, past), F32),
                            pltpu.SemaphoreType.DMA((2,)), pltpu.SemaphoreType.DMA((2,))]),
        compiler_params=_cparams(1),
        name="attend_sample",
    )(page_table, q_rows, mask, band, k_new_t, v_new_t, ck_t, cv_t)


def _merge_body(x_ref, ca_ref, att_ref, gm_ref, gt_ref, wco_ref, wao_ref, wgo_ref, wout_ref, post_ref, o_ref):
    d = x_ref.shape[1]
    gt = gt_ref[...].astype(F32)
    m = (gt[:, 0:d] * _dot(ca_ref[...], wco_ref[...])
         + gt[:, d:2 * d] * _dot(att_ref[...], wao_ref[...])
         + gt[:, 2 * d:3 * d] * _dot(gm_ref[...], wgo_ref[...]))
    o_ref[...] = x_ref[...] + _rms(_dot(m.astype(BF), wout_ref[...]), post_ref[...])


def _merge(x, ca, att, gm, gates, layer, w, *, tm):
    n, d = x.shape
    row = lambda i: (i, 0)
    return pl.pallas_call(
        _merge_body,
        out_shape=jax.ShapeDtypeStruct((n, d), F32),
        grid=(n // tm,),
        in_specs=[pl.BlockSpec((tm, d), row), pl.BlockSpec((tm, ca.shape[1]), row),
                  pl.BlockSpec((tm, att.shape[1]), row), pl.BlockSpec((tm, gm.shape[1]), row),
                  pl.BlockSpec((tm, gates.shape[1]), row),
                  _resident(w["wco"].shape[1:], layer), _resident(w["wao"].shape[1:], layer),
                  _resident(w["wgo"].shape[1:], layer), _resident(w["wout"].shape[1:], layer),
                  _resident((1, d), layer)],
        out_specs=pl.BlockSpec((tm, d), row),
        compiler_params=_cparams(1),
        name="merge",
    )(x, ca, att, gm, gates, w["wco"], w["wao"], w["wgo"], w["wout"], w["mix_post"])


def _band_body(rb_ref, bucket_ref, o_ref):
    h = pl.program_id(0)
    bucket = bucket_ref[...]
    far = rb_ref[N_BUCKETS - 1, h]
    tile = jnp.zeros(bucket.shape, F32)
    for bkt in range(N_BUCKETS):
        tile = jnp.where(bucket == bkt, rb_ref[bkt, h] - far, tile)
    o_ref[...] = tile * LOG2E


def _band_bias(rel_bias, dist):
    max_exact = N_BUCKETS // 2
    d = jnp.maximum(dist, 1).astype(F32)
    large = max_exact + (jnp.log(d / max_exact) / math.log(MAX_DISTANCE / max_exact)
                         * (N_BUCKETS - max_exact)).astype(jnp.int32)
    bucket = jnp.where(dist < max_exact, dist, jnp.minimum(large, N_BUCKETS - 1))
    r, c = dist.shape
    return pl.pallas_call(
        _band_body,
        out_shape=jax.ShapeDtypeStruct((N_HEADS, r, c), F32),
        grid=(N_HEADS,),
        in_specs=[pl.BlockSpec(memory_space=pltpu.SMEM), pl.BlockSpec((r, c), lambda h: (0, 0))],
        out_specs=pl.BlockSpec((None, r, c), lambda h: (h, 0, 0)),
        compiler_params=_cparams(1),
        name="band_bias",
    )(rel_bias, bucket)


def _prep_weights(p):
    d = p["w_in"].shape[1]
    cc = p["conv_w"].shape[2]
    cg = p["gmlp_ln_g"].shape[1]
    hq = N_HEADS * HEAD_DIM
    hk = N_KV_HEADS * HEAD_DIM
    hi = N_IDX_HEADS * IDX_DIM
    win = p["w_in"].astype(BF)
    o = 0
    cols = {}
    for name, wd in (("a", 2 * cc), ("q", hq), ("k", hk), ("v", hk), ("qi", hi), ("ki", IDX_DIM),
                     ("wi", N_IDX_HEADS), ("uv", 2 * cg), ("g", 3 * d)):
        cols[name] = win[:, :, o:o + wd]
        o += wd
    pad = jnp.zeros(cols["wi"].shape[:2] + (LANES - IDX_DIM - N_IDX_HEADS,), BF)
    vec = lambda a: a[:, None, :]
    kvw = [cols["k"], cols["v"], cols["ki"], cols["wi"], pad]
    return {
        "wa": cols["a"],
        "wrow_p": jnp.concatenate(kvw, axis=-1),
        "wrow_s": jnp.concatenate([cols["q"], cols["qi"]] + kvw, axis=-1),
        "wcol": jnp.swapaxes(jnp.concatenate([cols["q"], cols["qi"], cols["v"], cols["wi"]], axis=-1), 1, 2),
        "wuv": cols["uv"], "wg": cols["g"],
        "mix_pre": vec(p["mix_norm_pre"]), "mix_post": vec(p["mix_norm_post"]),
        "conv_w": p["conv_w"], "conv_b": vec(p["conv_b"]),
        "conv_ln_g": vec(p["conv_ln_g"]), "conv_ln_b": vec(p["conv_ln_b"]),
        "gmlp_ln_g": vec(p["gmlp_ln_g"]), "gmlp_ln_b": vec(p["gmlp_ln_b"]),
        "wco": p["w_conv_out"].astype(BF), "wao": p["w_attn_out"].astype(BF),
        "wgo": p["w_gmlp_out"].astype(BF), "wout": p["w_out"].astype(BF),
        "f1": (vec(p["ffn1_norm_pre"]), vec(p["ffn1_norm_post"]), p["ffn1_w_gate"].astype(BF),
               p["ffn1_w_up"].astype(BF), p["ffn1_w_down"].astype(BF)),
        "f2": (vec(p["ffn2_norm_pre"]), vec(p["ffn2_norm_post"]), p["ffn2_w_gate"].astype(BF),
               p["ffn2_w_up"].astype(BF), p["ffn2_w_down"].astype(BF)),
    }


def _gmlp_spatial(ws, bs, rows, n_seq):
    gd = LANES
    if n_seq is None:
        w = ws
        b = jnp.swapaxes(bs, 1, 2)
    else:
        t = rows // n_seq
        eye = jnp.eye(n_seq, dtype=ws.dtype)
        w = jnp.einsum("lgts,bc->lgtbsc", ws[:, :, :t, :t], eye).reshape(ws.shape[0], ws.shape[1], rows, rows)
        b = jnp.repeat(jnp.swapaxes(bs[:, :, :t], 1, 2), n_seq, axis=1)
    return w, jnp.repeat(b, gd, axis=2)


def kernel(x_prompt, x_sample, cache_k, cache_v, cache_idx_k, state_conv, page_table,
           ffn1_norm_pre, ffn1_norm_post, ffn1_w_gate, ffn1_w_up, ffn1_w_down,
           mix_norm_pre, mix_norm_post, w_in, conv_w, conv_b, conv_ln_g, conv_ln_b,
           w_conv_out, w_attn_out, rel_bias, gmlp_ln_g, gmlp_ln_b, gmlp_ws, gmlp_bs,
           w_gmlp_out, w_out, ffn2_norm_pre, ffn2_norm_post, ffn2_w_gate, ffn2_w_up, ffn2_w_down):
    params = dict(ffn1_norm_pre=ffn1_norm_pre, ffn1_norm_post=ffn1_norm_post, ffn1_w_gate=ffn1_w_gate,
                  ffn1_w_up=ffn1_w_up, ffn1_w_down=ffn1_w_down, mix_norm_pre=mix_norm_pre,
                  mix_norm_post=mix_norm_post, w_in=w_in, conv_w=conv_w, conv_b=conv_b,
                  conv_ln_g=conv_ln_g, conv_ln_b=conv_ln_b, w_conv_out=w_conv_out, w_attn_out=w_attn_out,
                  gmlp_ln_g=gmlp_ln_g, gmlp_ln_b=gmlp_ln_b, w_gmlp_out=w_gmlp_out, w_out=w_out,
                  ffn2_norm_pre=ffn2_norm_pre, ffn2_norm_post=ffn2_norm_post, ffn2_w_gate=ffn2_w_gate,
                  ffn2_w_up=ffn2_w_up, ffn2_w_down=ffn2_w_down)
    depth = w_in.shape[0]
    nb, seq, d = x_prompt.shape
    db, t_new, _ = x_sample.shape
    page = cache_k.shape[2]
    n_pages = page_table.shape[1]
    past = n_pages * page
    n_s = db * t_new
    cc = conv_w.shape[2]
    cg = gmlp_ln_g.shape[1]

    w = _prep_weights(params)
    wp = dict(w)
    wp["ws"], wp["bsb"] = _gmlp_spatial(gmlp_ws, gmlp_bs, GMLP_CHUNK, None)
    wsm = dict(w)
    wsm["ws"], wsm["bsb"] = _gmlp_spatial(gmlp_ws, gmlp_bs, n_s, db)

    tq = 256
    tm_p = 512 if seq % 512 == 0 else 256
    k_sel_p = min(TOPK_MAX, seq // 4)
    k_sel_s = min(TOPK_MAX, (past + t_new) // 4)

    r = jnp.arange(tq, dtype=jnp.int32)[None, :]
    c = jnp.arange(2 * tq, dtype=jnp.int32)[:, None]
    band_p = _band_bias(rel_bias, jnp.maximum(tq + r - c, 0))
    band_p = band_p.reshape(N_KV_HEADS, GROUP_SIZE, 2, tq, tq).transpose(0, 2, 3, 1, 4)
    band_p = band_p.reshape(N_KV_HEADS, 2, tq, GROUP_SIZE * tq)
    r8 = jnp.minimum(jnp.arange(8, dtype=jnp.int32), t_new - 1)[:, None]
    c2 = jnp.arange(2 * page, dtype=jnp.int32)[None, :]
    band_s = _band_bias(rel_bias, jnp.maximum(r8 + page - c2, 0)).reshape(N_HEADS * 8, 2 * page)

    ck_t = jnp.transpose(cache_k, (0, 1, 3, 4, 2)).reshape(depth, -1, N_KV_HEADS * HEAD_DIM, page)
    cv_t = jnp.transpose(cache_v, (0, 1, 3, 4, 2)).reshape(depth, -1, N_KV_HEADS * HEAD_DIM, page)
    cki_t = jnp.transpose(cache_idx_k, (0, 1, 3, 2))
    state_t = jnp.swapaxes(state_conv, 1, 2)

    xp = x_prompt.reshape(nb * seq, d)
    xs = jnp.swapaxes(x_sample, 0, 1).reshape(n_s, d)

    def heads_rows(a, width):
        a = a.reshape(t_new, db, -1, width).transpose(1, 2, 0, 3)
        a = jnp.pad(a, ((0, 0), (0, 0), (0, 8 - t_new), (0, 0)), mode="edge")
        return a.reshape(db, -1, width)

    outs = {k: [] for k in ("kp", "vp", "kip", "cp", "ks", "vs", "kis", "cs", "gv")}
    for l in range(depth):
        xp = _ffn(xp, l, *w["f1"], tm=tm_p)
        mi = _mix_in(xp, l, wp, tm=tm_p, n_seq=nb, seq_tiles=seq // tm_p, sample=False)
        att = _attn_prompt(mi, band_p, n_seq=nb, seq=seq, tq=tq, k_sel=k_sel_p)
        xp = _merge(xp, mi["ca"], att, mi["gm"], mi["gates"], l, w, tm=tm_p)
        xp = _ffn(xp, l, *w["f2"], tm=tm_p)
        kv = mi["kv"].reshape(nb, seq, 2, N_KV_HEADS, HEAD_DIM)
        outs["kp"].append(kv[:, :, 0])
        outs["vp"].append(kv[:, :, 1])
        outs["kip"].append(mi["kw"][:, :IDX_DIM].reshape(nb, seq, IDX_DIM))
        outs["cp"].append(mi["conv_state"])

        xs = _ffn(xs, l, *w["f1"], tm=n_s)
        ms = _mix_in(xs, l, wsm, tm=n_s, n_seq=db, seq_tiles=1, sample=True, state=state_t[l])
        kv_s = jnp.swapaxes(ms["kv"].reshape(t_new, db, 2, N_KV_HEADS * HEAD_DIM), 0, 1)
        ki_s = jnp.swapaxes(ms["kw"][:, :IDX_DIM].reshape(t_new, db, IDX_DIM), 0, 1)
        qi_rows = heads_rows(ms["qi"], IDX_DIM)
        wq_rows = heads_rows(jnp.broadcast_to(
            ms["kw"][:, IDX_DIM:IDX_DIM + N_IDX_HEADS, None], (n_s, N_IDX_HEADS, LANES)
        ).reshape(n_s, N_IDX_HEADS * LANES), LANES)
        qh = heads_rows(ms["q"], HEAD_DIM)
        zero = jnp.zeros_like(qh)
        first = (jnp.arange(N_HEADS * 8) < GROUP_SIZE * 8)[None, :, None]
        q_rows = jnp.concatenate([jnp.where(first, qh, zero), jnp.where(first, zero, qh)], axis=-1)
        new_t = lambda a: jnp.pad(jnp.swapaxes(a, 1, 2), ((0, 0), (0, 0), (0, page - t_new))).astype(BF)
        o = _attn_sample(page_table, qi_rows, wq_rows, q_rows, new_t(ki_s), new_t(kv_s[:, :, 0]),
                         new_t(kv_s[:, :, 1]), band_s, cki_t, ck_t, cv_t, layer=l, k_sel=k_sel_s)
        o = o.reshape(db, N_HEADS, 8, N_KV_HEADS, HEAD_DIM)[:, :, :t_new]
        o = jnp.concatenate([o[:, :GROUP_SIZE, :, 0], o[:, GROUP_SIZE:, :, 1]], axis=1)
        att_s = o.transpose(2, 0, 1, 3).reshape(n_s, N_HEADS * HEAD_DIM).astype(BF)
        xs = _merge(xs, ms["ca"], att_s, ms["gm"], ms["gates"], l, w, tm=n_s)
        xs = _ffn(xs, l, *w["f2"], tm=n_s)
        outs["ks"].append(kv_s[:, :, 0].reshape(db, t_new, N_KV_HEADS, HEAD_DIM))
        outs["vs"].append(kv_s[:, :, 1].reshape(db, t_new, N_KV_HEADS, HEAD_DIM))
        outs["kis"].append(ki_s)
        outs["cs"].append(jnp.swapaxes(ms["conv_state"], 0, 1))
        outs["gv"].append(jnp.swapaxes(ms["gmlp_v"].reshape(t_new, db, cg), 0, 1))

    yp = xp.reshape(nb, seq, d)
    ys = jnp.swapaxes(xs.reshape(t_new, db, d), 0, 1)
    st = lambda k: jnp.stack(outs[k])
    return (yp, ys, st("kp"), st("vp"), st("kip"), st("cp"),
            st("ks"), st("vs"), st("kis"), st("cs"), st("gv"))
```

```python
import functools
import math

import numpy as np
import jax
import jax.numpy as jnp
from jax import lax
from jax.experimental import pallas as pl
from jax.experimental.pallas import tpu as pltpu

F32 = jnp.float32
BF = jnp.bfloat16

N_HEADS = 8
N_KV_HEADS = 2
HEAD_DIM = 64
GROUP_SIZE = N_HEADS // N_KV_HEADS
N_IDX_HEADS = 8
IDX_DIM = 64
TOPK_MAX = 256
N_BUCKETS = 32
MAX_DISTANCE = 128
N_GMLP_GROUPS = 4
GMLP_CHUNK = 128
EPS = 1e-6

LANES = 128
MASKED = -1e30
M_INIT = -1e29
LOG2E = 1.4426950408889634
F32_LOWEST = -3.4028234663852886e38
VMEM_LIMIT = 56 * 1024 * 1024

NT_DIMS = (((1,), (1,)), ((), ()))


def _cparams(n_axes):
    return pltpu.CompilerParams(dimension_semantics=("arbitrary",) * n_axes,
                                vmem_limit_bytes=VMEM_LIMIT)


def _resident(shape, layer=None):
    nd = len(shape)
    if layer is None:
        return pl.BlockSpec(shape, lambda *_: (0,) * nd, pipeline_mode=pl.Buffered(1))
    return pl.BlockSpec((None,) + tuple(shape), lambda *_: (layer,) + (0,) * nd,
                        pipeline_mode=pl.Buffered(1))


def _rms(x, g):
    return x * lax.rsqrt(jnp.mean(x * x, axis=-1, keepdims=True) + EPS) * g


def _layer_norm(x, g, b):
    mu = jnp.mean(x, axis=-1, keepdims=True)
    xc = x - mu
    return xc * lax.rsqrt(jnp.mean(xc * xc, axis=-1, keepdims=True) + EPS) * g + b


def _dot(a, b):
    return jnp.dot(a, b, preferred_element_type=F32)


def _dot_nt(a, bt):
    return lax.dot_general(a, bt, NT_DIMS, preferred_element_type=F32)


def _sigmoid(x):
    return 0.5 * jnp.tanh(0.5 * x) + 0.5


def _silu(x):
    return x * _sigmoid(x)


FF_CHUNK = 256


def _ffn_half_step(x, pre_ref, post_ref, wg_ref, wu_ref, wd_ref):
    h = _rms(x, pre_ref[...]).astype(BF)
    acc = jnp.zeros(x.shape, F32)
    for c in range(wg_ref.shape[1] // FF_CHUNK):
        sl = slice(c * FF_CHUNK, (c + 1) * FF_CHUNK)
        a = _silu(_dot(h, wg_ref[:, sl])) * _dot(h, wu_ref[:, sl])
        acc = acc + _dot(a.astype(BF), wd_ref[sl, :])
    return x + 0.5 * _rms(acc, post_ref[...])


def _ffn_body(x_ref, pre_ref, post_ref, wg_ref, wu_ref, wd_ref, o_ref):
    o_ref[...] = _ffn_half_step(x_ref[...], pre_ref, post_ref, wg_ref, wu_ref, wd_ref)


def _ffn(x, layer, pre, post, wg, wu, wd, *, tm):
    n, d = x.shape
    dff = wg.shape[2]
    return pl.pallas_call(
        _ffn_body,
        out_shape=jax.ShapeDtypeStruct((n, d), F32),
        grid=(n // tm,),
        in_specs=[pl.BlockSpec((tm, d), lambda i: (i, 0)),
                  _resident((1, d), layer), _resident((1, d), layer),
                  _resident((d, dff), layer), _resident((d, dff), layer), _resident((dff, d), layer)],
        out_specs=pl.BlockSpec((tm, d), lambda i: (i, 0)),
        compiler_params=_cparams(1),
        name="ffn",
    )(x, pre, post, wg, wu, wd)


_MIX_IN = ["x", "pre", "wa", "wrow", "wuv", "wg", "cw", "cb", "clg", "clb", "glg", "glb", "ws", "bsb"]
_Q_SCALE = HEAD_DIM ** -0.5 * LOG2E
_HQ = N_HEADS * HEAD_DIM
_HI = N_IDX_HEADS * IDX_DIM
_HKV = N_KV_HEADS * HEAD_DIM


def _mix_in_body(*refs, names, tm, conv_w, sample, n_seq):
    r = dict(zip(names, refs))
    ca_ref, gm_ref, cs_ref, xin_ref = r["ca"], r["gm"], r["conv_state"], r["xin"]
    cw_ref, cb_ref, clg_ref, clb_ref = r["cw"], r["cb"], r["clg"], r["clb"]
    glg_ref, glb_ref, ws_ref, bsb_ref = r["glg"], r["glb"], r["ws"], r["bsb"]

    h = _rms(r["x"][...], r["pre"][...]).astype(BF)

    def projections():
        row = _dot_nt(h, r["wrow"][...])
        if sample:
            r["q"][...] = (row[:, 0:_HQ] * _Q_SCALE).astype(BF)
            r["qi"][...] = row[:, _HQ:_HQ + _HI].astype(BF)
            o = _HQ + _HI
        else:
            col = _dot_nt(r["wcol"][...], h)
            r["qt"][...] = (col[0:_HQ] * _Q_SCALE).astype(BF)
            r["qit"][...] = col[_HQ:_HQ + _HI].astype(BF)
            vt = col[_HQ + _HI:_HQ + _HI + _HKV].astype(BF)
            tq = r["vt"].shape[2]
            for c in range(tm // tq):
                r["vt"][c] = vt[:, c * tq:(c + 1) * tq]
            r["wit"][...] = col[_HQ + _HI + _HKV:_HQ + _HI + _HKV + N_IDX_HEADS]
            r["ki"][...] = row[:, 2 * _HKV:2 * _HKV + IDX_DIM].astype(BF)
            for g in range(N_KV_HEADS):
                r["kg"][g] = row[:, g * HEAD_DIM:(g + 1) * HEAD_DIM].astype(BF)
            o = 0
        r["kv"][...] = row[:, o:o + 2 * _HKV]
        r["kw"][...] = row[:, o + 2 * _HKV:o + 2 * _HKV + LANES]
        r["gates"][...] = _sigmoid(_dot_nt(h, r["wg"][...])).astype(BF)

    wa_ref, wuv_ref = r["wa"], r["wuv"]

    a = _dot_nt(h, wa_ref[...])
    uv = jax.nn.gelu(_dot_nt(h, wuv_ref[...]))
    cc = a.shape[1] // 2
    glu = a[:, :cc] * _sigmoid(a[:, cc:])
    pre_rows = conv_w - 1
    if sample:
        n_t = tm // n_seq
        xin_ref[0:pre_rows] = r["state"][...]
        for t in range(n_t):
            xin_ref[pre_rows + t] = glu[t * n_seq:(t + 1) * n_seq, :]
        ys = []
        for t in range(n_t):
            y = jnp.zeros((n_seq, cc), F32)
            for j in range(conv_w):
                y = y + xin_ref[t + j] * cw_ref[j:j + 1, :]
            ys.append(y)
        y = jnp.concatenate(ys, axis=0)
        cs_ref[...] = xin_ref[n_t:n_t + pre_rows]
        ca_ref[...] = _silu(_layer_norm(y + cb_ref[...], clg_ref[...], clb_ref[...])).astype(BF)
    else:
        halo = 32
        @pl.when(pl.program_id(1) == 0)
        def _():
            xin_ref[0:halo, :] = jnp.zeros((halo, cc), F32)
        xin_ref[halo:halo + tm, :] = glu
        rb = 64
        off = halo - pre_rows
        sh_ref = r["shifted"]
        for b in range(8):
            n_rows = tm + 8 * (len(range(b, conv_w, 8)) - 1)
            sh_ref[b, 0:n_rows, :] = xin_ref[off + b:off + b + n_rows, :]
        cs_ref[...] = xin_ref[halo + tm - pre_rows:halo + tm, :]
        xin_ref[0:halo, :] = xin_ref[tm:tm + halo, :]

    cg = uv.shape[1] // 2
    u = uv[:, :cg]
    vn = _layer_norm(uv[:, cg:], glg_ref[...], glb_ref[...])
    if sample:
        r["gmlp_v"][...] = vn
    vb = vn.astype(BF)

    projections()

    if not sample:
        for r0 in range(0, tm, rb):
            y = jnp.zeros((rb, cc), F32)
            for j in range(conv_w):
                y = y + sh_ref[j % 8, r0 + j - j % 8:r0 + j - j % 8 + rb, :] * cw_ref[j:j + 1, :]
            y = _silu(_layer_norm(y + cb_ref[...], clg_ref[...], clb_ref[...]))
            ca_ref[r0:r0 + rb, :] = y.astype(BF)

    gd = cg // N_GMLP_GROUPS
    chunk = ws_ref.shape[-1]
    ri = lax.broadcasted_iota(jnp.int32, (chunk, chunk), 0)
    ci = lax.broadcasted_iota(jnp.int32, (chunk, chunk), 1)
    wsm = [jnp.where(ci <= ri, ws_ref[g], 0.0).astype(BF) for g in range(N_GMLP_GROUPS)]
    for n in range(tm // chunk):
        rs = slice(n * chunk, (n + 1) * chunk)
        mixed = jnp.concatenate(
            [_dot(wsm[g], vb[rs, g * gd:(g + 1) * gd]) for g in range(N_GMLP_GROUPS)], axis=1)
        gm_ref[rs, :] = (u[rs, :] * (mixed + bsb_ref[...])).astype(BF)


def _mix_in(x, layer, w, *, tm, n_seq, seq_tiles, sample, state=None):
    n, d = x.shape
    conv_w = w["conv_w"].shape[1]
    cc = w["conv_w"].shape[2]
    cg = w["gmlp_ln_g"].shape[2]
    pre_rows = conv_w - 1
    if sample:
        grid = (1,)
        row = lambda i: (0, 0)
        res = lambda shape: _resident(shape, layer)
        cs_shape = (pre_rows, n_seq, cc)
        cs_spec = pl.BlockSpec(cs_shape, lambda i: (0, 0, 0))
        scratch = {"xin": pltpu.VMEM((pre_rows + tm // n_seq, n_seq, cc), F32)}
    else:
        grid = (n_seq, seq_tiles)
        row = lambda b, t: (b * seq_tiles + t, 0)
        res = lambda shape: _resident(shape, layer)
        cs_shape = (n_seq, pre_rows, cc)
        cs_spec = pl.BlockSpec((None, pre_rows, cc), lambda b, t: (b, 0, 0))
        scratch = {"xin": pltpu.VMEM((32 + tm, cc), F32),
                   "shifted": pltpu.VMEM((8, tm + 8 * ((conv_w - 1) // 8), cc), F32)}

    wrow = w["wrow_s"] if sample else w["wrow_p"]
    in_names = list(_MIX_IN)
    args = [x, w["mix_pre"], w["wa"], wrow, w["wuv"], w["wg"], w["conv_w"], w["conv_b"], w["conv_ln_g"],
            w["conv_ln_b"], w["gmlp_ln_g"], w["gmlp_ln_b"], w["ws"], w["bsb"]]
    in_specs = [pl.BlockSpec((tm, d), row)] + [res(a.shape[1:]) for a in args[1:]]
    if sample:
        in_names.append("state")
        in_specs.append(pl.BlockSpec(cs_shape, lambda i: (0, 0, 0)))
        args.append(state)
    else:
        in_names.append("wcol")
        in_specs.append(res(w["wcol"].shape[1:]))
        args.append(w["wcol"])

    rows = lambda wd, dt: ((n, wd), (tm, wd), row, dt)
    outs = {"kv": rows(2 * _HKV, F32), "kw": rows(LANES, F32), "ca": rows(cc, BF), "gm": rows(cg, BF),
            "gates": rows(w["wg"].shape[1], BF), "conv_state": (cs_shape, cs_spec.block_shape, cs_spec.index_map, F32)}
    if sample:
        outs.update(q=rows(_HQ, BF), qi=rows(_HI, BF), gmlp_v=rows(cg, F32))
    else:
        tq = 256
        cols = lambda ht, dt: ((ht, n), (ht, tm), lambda b, t: (0, b * seq_tiles + t), dt)
        outs.update(qt=cols(_HQ, BF), qit=cols(_HI, BF), wit=cols(N_IDX_HEADS, F32), ki=rows(IDX_DIM, BF),
                    kg=((N_KV_HEADS, n, HEAD_DIM), (N_KV_HEADS, tm, HEAD_DIM),
                        lambda b, t: (0, b * seq_tiles + t, 0), BF),
                    vt=((n // tq, _HKV, tq), (tm // tq, _HKV, tq), lambda b, t: (b * seq_tiles + t, 0, 0), BF))
    out_names = list(outs)

    res_out = pl.pallas_call(
        functools.partial(_mix_in_body, names=in_names + out_names + list(scratch), tm=tm, conv_w=conv_w,
                          sample=sample, n_seq=n_seq),
        out_shape=[jax.ShapeDtypeStruct(outs[k][0], outs[k][3]) for k in out_names],
        grid=grid, in_specs=in_specs,
        out_specs=[pl.BlockSpec(outs[k][1], outs[k][2]) for k in out_names],
        scratch_shapes=list(scratch.values()), compiler_params=_cparams(len(grid)),
        name="mix_in_sample" if sample else "mix_in",
    )(*args)
    return dict(zip(out_names, res_out))


def _ordered_bits_to_float(u):
    t = u ^ jnp.int32(-2 ** 31)
    fb = t ^ (lax.shift_right_arithmetic(t, 31) & jnp.int32(0x7FFFFFFF))
    return lax.bitcast_convert_type(fb, F32)


def _row_total(cnt, ones_bf):
    return _dot(cnt.astype(BF), ones_bf)


def _select_topk(sc_ref, nk, rows, width, k_sel):
    nh = width // LANES
    ones_bf = jnp.ones((LANES, LANES), BF)
    kf = float(k_sel)

    def halves(x):
        return [x[:, i * LANES:(i + 1) * LANES] for i in range(nh)]

    def count(pred):
        def body(c, cnt):
            for xh in halves(sc_ref[c]):
                cnt = cnt + jnp.where(pred(xh), 1.0, 0.0)
            return cnt
        return _row_total(lax.fori_loop(0, nk, body, jnp.zeros((rows, LANES), F32)), ones_bf)

    def bit_body(i, u):
        cand = u | jnp.left_shift(jnp.int32(1), 31 - i)
        thr = _ordered_bits_to_float(cand)
        return jnp.where(count(lambda xh: xh >= thr) >= kf, cand, u)

    u = lax.fori_loop(0, 32, bit_body, jnp.zeros((rows, LANES), jnp.int32))
    thr = _ordered_bits_to_float(u)
    thr = jnp.where(thr >= F32_LOWEST, thr, F32_LOWEST)

    n_ge = count(lambda xh: xh >= thr)

    @pl.when(jnp.max(n_ge) > kf)
    def _():
        need = kf - count(lambda xh: xh > thr)
        ri = lax.broadcasted_iota(jnp.int32, (LANES, LANES), 0)
        ci = lax.broadcasted_iota(jnp.int32, (LANES, LANES), 1)
        before = jnp.where(ri < ci, 1.0, 0.0).astype(BF)

        def fix(c, seen):
            out = []
            for xh in halves(sc_ref[c]):
                tie = xh == thr
                tf = jnp.where(tie, 1.0, 0.0).astype(BF)
                rank = seen + _dot(tf, before)
                out.append(jnp.where(jnp.where(tie, rank, -1.0) >= need, -jnp.inf, xh))
                seen = seen + _dot(tf, ones_bf)
            sc_ref[c] = out[0] if nh == 1 else jnp.concatenate(out, axis=1)
            return seen

        lax.fori_loop(0, nk, fix, jnp.zeros((rows, LANES), F32))

    def to_mask(c, carry):
        x = sc_ref[c]
        thr_w = thr if nh == 1 else jnp.concatenate([thr] * nh, axis=1)
        sc_ref[c] = jnp.where(x >= thr_w, 0.0, MASKED)
        return carry

    lax.fori_loop(0, nk, to_mask, 0)


def _select_topk_cols(sc_ref, nk, n_keys, n_q, k_sel):
    kf = float(k_sel)
    groups = n_keys // 8

    def count(pred):
        def body(c, cnt):
            x = sc_ref[c].reshape(groups, 8, n_q)
            return cnt + jnp.sum(jnp.where(pred(x), 1.0, 0.0), axis=0)
        cnt = lax.fori_loop(0, nk, body, jnp.zeros((8, n_q), F32))
        return jnp.sum(cnt, axis=0, keepdims=True)

    def rows8(v):
        return jnp.broadcast_to(v, (8, n_q))[None]

    def bit_body(i, u):
        cand = u | jnp.left_shift(jnp.int32(1), 31 - i)
        thr8 = rows8(_ordered_bits_to_float(cand))
        return jnp.where(count(lambda x: x >= thr8) >= kf, cand, u)

    u = lax.fori_loop(0, 32, bit_body, jnp.zeros((1, n_q), jnp.int32))
    thr = _ordered_bits_to_float(u)
    thr = jnp.where(thr >= F32_LOWEST, thr, F32_LOWEST)
    thr8 = rows8(thr)

    @pl.when(jnp.max(count(lambda x: x >= thr8)) > kf)
    def _():
        need = kf - count(lambda x: x > thr8)
        ri = lax.broadcasted_iota(jnp.int32, (n_keys, n_keys), 0)
        ci = lax.broadcasted_iota(jnp.int32, (n_keys, n_keys), 1)
        before = jnp.where(ci < ri, 1.0, 0.0).astype(BF)

        def fix(c, seen):
            x = sc_ref[c]
            tie = x == thr
            tf = jnp.where(tie, 1.0, 0.0)
            rank = seen + _dot(before, tf.astype(BF))
            sc_ref[c] = jnp.where(jnp.where(tie, rank, -1.0) >= need, -jnp.inf, x)
            return seen + jnp.sum(tf, axis=0, keepdims=True)

        lax.fori_loop(0, nk, fix, jnp.zeros((1, n_q), F32))

    def to_mask(c, carry):
        sc_ref[c] = jnp.where(sc_ref[c] >= thr, 0.0, MASKED)
        return carry

    lax.fori_loop(0, nk, to_mask, 0)


def _attn_prompt_body(qt_ref, qit_ref, wit_ref, ki_ref, kg_ref, vt_ref, band_ref, o_ref, sc_ref,
                      *, tq, k_sel, heads_per_stream):
    j = pl.program_id(1)
    nk = j + 1

    def heads_along_lanes(x, heads, width):
        return jnp.concatenate([x[h * width:(h + 1) * width, :] for h in heads], axis=1)

    qi_all = heads_along_lanes(qit_ref[...], range(N_IDX_HEADS), IDX_DIM)
    w_all = heads_along_lanes(wit_ref[...], range(N_IDX_HEADS), 1)
    halves = tq // LANES
    q_pos = lax.broadcasted_iota(jnp.int32, (tq, LANES), 1) + j * tq

    def score_chunk(c, carry):
        k0 = pl.multiple_of(c * tq, tq)
        kic = ki_ref[pl.ds(k0, tq), :]
        k_pos = lax.broadcasted_iota(jnp.int32, (tq, LANES), 0) + k0
        for i in range(halves):
            acc = None
            for h in range(N_IDX_HEADS):
                ls = slice(h * tq + i * LANES, h * tq + (i + 1) * LANES)
                term = jnp.maximum(_dot(kic, qi_all[:, ls]), 0.0) * w_all[:, ls]
                acc = term if acc is None else acc + term
            sc_ref[c, :, i * LANES:(i + 1) * LANES] = jnp.where(k_pos <= q_pos + i * LANES, acc, -jnp.inf)
        return carry

    lax.fori_loop(0, nk, score_chunk, 0)
    _select_topk_cols(sc_ref, nk, tq, tq, k_sel)

    qt = qt_ref[...]
    n_streams = N_HEADS // heads_per_stream
    per_group = GROUP_SIZE // heads_per_stream
    wide = heads_per_stream * tq
    q_s = [heads_along_lanes(qt, range(i * heads_per_stream, (i + 1) * heads_per_stream), HEAD_DIM)
           for i in range(n_streams)]

    def chunk_step(c, carry, bias):
        k0 = pl.multiple_of(c * tq, tq)
        mask = jnp.concatenate([sc_ref[c]] * heads_per_stream, axis=1)
        logits = []
        for i in range(n_streams):
            g = i // per_group
            s = _dot(kg_ref[g, pl.ds(k0, tq), :], q_s[i]) + mask
            if bias is not None:
                s = s + bias(g, slice((i % per_group) * wide, (i % per_group + 1) * wide))
            logits.append(s)
        stats = []
        for i in range(n_streams):
            m, l, _ = carry[i]
            m_new = jnp.maximum(m, jnp.max(logits[i], axis=0, keepdims=True))
            alpha = jnp.exp2(m - m_new)
            p = jnp.exp2(logits[i] - m_new)
            stats.append((m_new, alpha, alpha * l + jnp.sum(p, axis=0, keepdims=True), p.astype(BF)))
        out = []
        for i in range(n_streams):
            g = i // per_group
            m_new, alpha, l_new, p = stats[i]
            vt = vt_ref[c, g * HEAD_DIM:(g + 1) * HEAD_DIM, :]
            out.append((m_new, l_new, alpha * carry[i][2] + _dot(vt, p)))
        return out

    carry = [(jnp.full((1, wide), M_INIT, F32), jnp.zeros((1, wide), F32), jnp.zeros((HEAD_DIM, wide), F32))
             for _ in range(n_streams)]
    j_prev = jnp.maximum(j - 1, 0)
    prev_mask = jnp.where(j > 0, 0.0, MASKED)
    carry = lax.fori_loop(0, j_prev, lambda c, carry: chunk_step(c, carry, None), carry)
    carry = chunk_step(j_prev, carry, lambda g, ls: band_ref[g, 0, :, ls] + prev_mask)
    carry = chunk_step(j, carry, lambda g, ls: band_ref[g, 1, :, ls])
    o_t = []
    for m, l, acc in carry:
        o_s = acc / l
        o_t += [o_s[:, hh * tq:(hh + 1) * tq] for hh in range(heads_per_stream)]
    o_ref[...] = jnp.concatenate(o_t, axis=0).T.astype(BF)


def _attn_prompt(mi, band, *, n_seq, seq, tq, k_sel):
    nt = seq // tq
    col = lambda b, t: (0, b * nt + t)
    hd = N_HEADS * HEAD_DIM
    return pl.pallas_call(
        functools.partial(_attn_prompt_body, tq=tq, k_sel=k_sel, heads_per_stream=2),
        out_shape=jax.ShapeDtypeStruct((n_seq * seq, hd), BF),
        grid=(n_seq, nt),
        in_specs=[pl.BlockSpec((hd, tq), col), pl.BlockSpec((N_IDX_HEADS * IDX_DIM, tq), col),
                  pl.BlockSpec((N_IDX_HEADS, tq), col),
                  pl.BlockSpec((seq, IDX_DIM), lambda b, t: (b, 0)),
                  pl.BlockSpec((N_KV_HEADS, seq, HEAD_DIM), lambda b, t: (0, b, 0)),
                  pl.BlockSpec((nt, N_KV_HEADS * HEAD_DIM, tq), lambda b, t: (b, 0, 0)),
                  _resident(band.shape)],
        out_specs=pl.BlockSpec((tq, hd), lambda b, t: (b * nt + t, 0)),
        scratch_shapes=[pltpu.VMEM((nt, tq, tq), F32)],
        compiler_params=_cparams(2),
        name="attn_prompt",
    )(mi["qt"], mi["qit"], mi["wit"], mi["ki"], mi["kg"], mi["vt"], band)


def _page_stream(cache_ref, buf_ref, sem_ref, pt_ref, layer, n_pages, page):
    def copy(seq, slot, p):
        return pltpu.make_async_copy(cache_ref.at[layer, pt_ref[seq, p]],
                                     buf_ref.at[slot, :, p * page:(p + 1) * page], sem_ref.at[slot])

    def start(seq, slot):
        for p in range(n_pages):
            copy(seq, slot, p).start()

    def wait(seq, slot):
        for p in range(n_pages):
            copy(seq, slot, p).wait()

    return start, wait


def _double_buffered(streams):
    i = pl.program_id(0)
    slot = i % 2

    @pl.when(i == 0)
    def _():
        for start, _ in streams:
            start(0, 0)

    @pl.when(i + 1 < pl.num_programs(0))
    def _():
        for start, _ in streams:
            start(i + 1, 1 - slot)

    for _, wait in streams:
        wait(i, slot)
    return slot


def _idx_scores_body(pt_ref, qi_ref, wq_ref, kin_ref, cki_ref, o_ref, buf_ref, sem_ref,
                     *, layer, n_pages, page, chunk):
    slot = _double_buffered([_page_stream(cki_ref, buf_ref, sem_ref, pt_ref, layer, n_pages, page)])
    qi = qi_ref[...]
    wq = wq_ref[...]
    rows_t = o_ref.shape[0]
    past = n_pages * page

    def head_sum(s):
        w = s.shape[1]
        wgt = wq if w == LANES else jnp.concatenate([wq] * (w // LANES), axis=1)
        s = (jnp.maximum(s, 0.0) * wgt).reshape(N_IDX_HEADS, rows_t, w)
        acc = s[0]
        for h in range(1, N_IDX_HEADS):
            acc = acc + s[h]
        return acc

    for c in range(past // chunk):
        cs = slice(c * chunk, (c + 1) * chunk)
        o_ref[:, cs] = head_sum(_dot(qi, buf_ref[slot, :, cs].astype(BF)))
    t = lax.broadcasted_iota(jnp.int32, (rows_t, page), 0)
    cc = lax.broadcasted_iota(jnp.int32, (rows_t, page), 1)
    o_ref[:, past:past + page] = jnp.where(cc <= t, head_sum(_dot(qi, kin_ref[...])), -jnp.inf)


def _select_body(sc_ref, o_ref, *, k_sel):
    o_ref[...] = sc_ref[...]
    _select_topk(o_ref, 1, o_ref.shape[1], o_ref.shape[2], k_sel)


def _attend_sample_body(pt_ref, q_ref, mask_ref, band_ref, kn_ref, vn_ref, ck_ref, cv_ref, o_ref,
                        kbuf_ref, vbuf_ref, s_ref, ksem_ref, vsem_ref, *, layer, n_pages, page, chunk):
    slot = _double_buffered([_page_stream(ck_ref, kbuf_ref, ksem_ref, pt_ref, layer, n_pages, page),
                             _page_stream(cv_ref, vbuf_ref, vsem_ref, pt_ref, layer, n_pages, page)])
    q = q_ref[...]
    rows = q.shape[0]
    rows_t = mask_ref.shape[0]
    past = n_pages * page
    n_chunks = past // chunk

    def add_mask(s, m):
        w = s.shape[1]
        return (s.reshape(N_HEADS, rows_t, w) + m[None]).reshape(rows, w)

    def lane_fold(x, fn, init):
        for i in range(x.shape[1] // LANES):
            init = fn(init, x[:, i * LANES:(i + 1) * LANES])
        return init

    mx = jnp.full((rows, LANES), M_INIT, F32)
    for c in range(n_chunks):
        cs = slice(c * chunk, (c + 1) * chunk)
        s = add_mask(_dot(q, kbuf_ref[slot, :, cs].astype(BF)), mask_ref[:, cs])
        if c == n_chunks - 1:
            s = jnp.concatenate([s[:, :chunk - page], s[:, chunk - page:] + band_ref[:, 0:page]], axis=1)
        s_ref[:, cs] = s
        mx = lane_fold(s, jnp.maximum, mx)
    s_new = add_mask(_dot(q, kn_ref[...]), mask_ref[:, past:past + page]) + band_ref[:, page:2 * page]
    m = jnp.max(jnp.maximum(mx, s_new), axis=1, keepdims=True)

    p_new = jnp.exp2(s_new - m)
    l = p_new
    acc = lax.dot_general(p_new.astype(BF), vn_ref[...], NT_DIMS, preferred_element_type=F32)
    for c in range(n_chunks):
        cs = slice(c * chunk, (c + 1) * chunk)
        p = jnp.exp2(s_ref[:, cs] - m)
        l = lane_fold(p, jnp.add, l)
        acc = acc + lax.dot_general(p.astype(BF), vbuf_ref[slot, :, cs].astype(BF), NT_DIMS,
                                    preferred_element_type=F32)
    o_ref[...] = acc / jnp.sum(l, axis=1, keepdims=True)


def _attn_sample(page_table, qi_rows, wq_rows, q_rows, ki_new_t, k_new_t, v_new_t, band, cki_t, ck_t, cv_t,
                 *, layer, k_sel, n_new):
    n_seq, n_pages = page_table.shape
    page = ck_t.shape[3]
    rows = q_rows.shape[1]
    rows_t = rows // N_HEADS
    past = n_pages * page
    width = past + page
    chunk = 1024 if past % 1024 == 0 else page
    per_seq3 = lambda b, pt: (b, 0, 0)
    any_spec = pl.BlockSpec(memory_space=pl.ANY)

    scores = pl.pallas_call(
        functools.partial(_idx_scores_body, layer=layer, n_pages=n_pages, page=page, chunk=chunk),
        out_shape=jax.ShapeDtypeStruct((n_seq, rows_t, width), F32),
        grid_spec=pltpu.PrefetchScalarGridSpec(
            num_scalar_prefetch=1, grid=(n_seq,),
            in_specs=[pl.BlockSpec((None, rows, IDX_DIM), per_seq3),
                      pl.BlockSpec((None, rows, LANES), per_seq3),
                      pl.BlockSpec((None, IDX_DIM, page), per_seq3),
                      any_spec],
            out_specs=pl.BlockSpec((None, rows_t, width), per_seq3),
            scratch_shapes=[pltpu.VMEM((2, IDX_DIM, past), F32), pltpu.SemaphoreType.DMA((2,))]),
        compiler_params=_cparams(1),
        name="idx_scores_sample",
    )(page_table, qi_rows, wq_rows, ki_new_t, cki_t)

    n_sel = n_seq * n_new
    sel_rows = 64 if n_sel % 64 == 0 else n_sel
    mask = pl.pallas_call(
        functools.partial(_select_body, k_sel=k_sel),
        out_shape=jax.ShapeDtypeStruct((1, n_sel, width), F32),
        grid=(n_sel // sel_rows,),
        in_specs=[pl.BlockSpec((1, sel_rows, width), lambda i: (0, i, 0))],
        out_specs=pl.BlockSpec((1, sel_rows, width), lambda i: (0, i, 0)),
        compiler_params=_cparams(1),
        name="select_sample",
    )(scores[:, :n_new].reshape(1, n_sel, width)).reshape(n_seq, n_new, width)
    mask = jnp.pad(mask, ((0, 0), (0, rows_t - n_new), (0, 0)), mode="edge")

    return pl.pallas_call(
        functools.partial(_attend_sample_body, layer=layer, n_pages=n_pages, page=page, chunk=chunk),
        out_shape=jax.ShapeDtypeStruct((n_seq, rows, LANES), F32),
        grid_spec=pltpu.PrefetchScalarGridSpec(
            num_scalar_prefetch=1, grid=(n_seq,),
            in_specs=[pl.BlockSpec((None, rows, LANES), per_seq3),
                      pl.BlockSpec((None, rows_t, width), per_seq3),
                      pl.BlockSpec(band.shape, lambda b, pt: (0, 0), pipeline_mode=pl.Buffered(1)),
                      pl.BlockSpec((None, LANES, page), per_seq3),
                      pl.BlockSpec((None, LANES, page), per_seq3),
                      any_spec, any_spec],
            out_specs=pl.BlockSpec((None, rows, LANES), per_seq3),
            scratch_shapes=[pltpu.VMEM((2, LANES, past), F32), pltpu.VMEM((2, LANES, past), F32),
                            pltpu.VMEM((rows, past), F32),
                            pltpu.SemaphoreType.DMA((2,)), pltpu.SemaphoreType.DMA((2,))]),
        compiler_params=_cparams(1),
        name="attend_sample",
    )(page_table, q_rows, mask, band, k_new_t, v_new_t, ck_t, cv_t)


def _merge_ffn_body(x_ref, ca_ref, att_ref, gm_ref, gt_ref, wco_ref, wao_ref, wgo_ref, wout_ref, post_ref,
                    fpre_ref, fpost_ref, wg_ref, wu_ref, wd_ref, o_ref):
    d = x_ref.shape[1]
    gt = gt_ref[...].astype(F32)
    m = (gt[:, 0:d] * _dot(ca_ref[...], wco_ref[...])
         + gt[:, d:2 * d] * _dot(att_ref[...], wao_ref[...])
         + gt[:, 2 * d:3 * d] * _dot(gm_ref[...], wgo_ref[...]))
    x = x_ref[...] + _rms(_dot(m.astype(BF), wout_ref[...]), post_ref[...])
    o_ref[...] = _ffn_half_step(x, fpre_ref, fpost_ref, wg_ref, wu_ref, wd_ref)


def _merge_ffn(x, ca, att, gm, gates, layer, w, *, tm):
    n, d = x.shape
    row = lambda i: (i, 0)
    weights = [w["wco"], w["wao"], w["wgo"], w["wout"], w["mix_post"], *w["f2"]]
    return pl.pallas_call(
        _merge_ffn_body,
        out_shape=jax.ShapeDtypeStruct((n, d), F32),
        grid=(n // tm,),
        in_specs=[pl.BlockSpec((tm, d), row), pl.BlockSpec((tm, ca.shape[1]), row),
                  pl.BlockSpec((tm, att.shape[1]), row), pl.BlockSpec((tm, gm.shape[1]), row),
                  pl.BlockSpec((tm, gates.shape[1]), row)] + [_resident(a.shape[1:], layer) for a in weights],
        out_specs=pl.BlockSpec((tm, d), row),
        compiler_params=_cparams(1),
        name="merge_ffn",
    )(x, ca, att, gm, gates, *weights)


def _band_body(rb_ref, bucket_ref, o_ref):
    h = pl.program_id(0)
    bucket = bucket_ref[...]
    far = rb_ref[N_BUCKETS - 1, h]
    tile = jnp.zeros(bucket.shape, F32)
    for bkt in range(N_BUCKETS):
        tile = jnp.where(bucket == bkt, rb_ref[bkt, h] - far, tile)
    o_ref[...] = tile * LOG2E


def _band_bias(rel_bias, dist):
    max_exact = N_BUCKETS // 2
    d = jnp.maximum(dist, 1).astype(F32)
    large = max_exact + (jnp.log(d / max_exact) / math.log(MAX_DISTANCE / max_exact)
                         * (N_BUCKETS - max_exact)).astype(jnp.int32)
    bucket = jnp.where(dist < max_exact, dist, jnp.minimum(large, N_BUCKETS - 1))
    r, c = dist.shape
    return pl.pallas_call(
        _band_body,
        out_shape=jax.ShapeDtypeStruct((N_HEADS, r, c), F32),
        grid=(N_HEADS,),
        in_specs=[pl.BlockSpec(memory_space=pltpu.SMEM), pl.BlockSpec((r, c), lambda h: (0, 0))],
        out_specs=pl.BlockSpec((None, r, c), lambda h: (h, 0, 0)),
        compiler_params=_cparams(1),
        name="band_bias",
    )(rel_bias, bucket)


def _prep_weights(p):
    d = p["w_in"].shape[1]
    cc = p["conv_w"].shape[2]
    cg = p["gmlp_ln_g"].shape[1]
    hq = N_HEADS * HEAD_DIM
    hk = N_KV_HEADS * HEAD_DIM
    hi = N_IDX_HEADS * IDX_DIM
    win_t = jnp.swapaxes(p["w_in"], 1, 2).astype(BF)
    o = 0
    cols = {}
    for name, wd in (("a", 2 * cc), ("q", hq), ("k", hk), ("v", hk), ("qi", hi), ("ki", IDX_DIM),
                     ("wi", N_IDX_HEADS), ("uv", 2 * cg), ("g", 3 * d)):
        cols[name] = win_t[:, o:o + wd, :]
        o += wd
    pad = jnp.zeros((win_t.shape[0], LANES - IDX_DIM - N_IDX_HEADS, d), BF)
    vec = lambda a: a[:, None, :]
    kvw = [cols["k"], cols["v"], cols["ki"], cols["wi"], pad]
    return {
        "wa": cols["a"],
        "wrow_p": jnp.concatenate(kvw, axis=1),
        "wrow_s": jnp.concatenate([cols["q"], cols["qi"]] + kvw, axis=1),
        "wcol": jnp.concatenate([cols["q"], cols["qi"], cols["v"], cols["wi"]], axis=1),
        "wuv": cols["uv"], "wg": cols["g"],
        "mix_pre": vec(p["mix_norm_pre"]), "mix_post": vec(p["mix_norm_post"]),
        "conv_w": p["conv_w"], "conv_b": vec(p["conv_b"]),
        "conv_ln_g": vec(p["conv_ln_g"]), "conv_ln_b": vec(p["conv_ln_b"]),
        "gmlp_ln_g": vec(p["gmlp_ln_g"]), "gmlp_ln_b": vec(p["gmlp_ln_b"]),
        "wco": p["w_conv_out"].astype(BF), "wao": p["w_attn_out"].astype(BF),
        "wgo": p["w_gmlp_out"].astype(BF), "wout": p["w_out"].astype(BF),
        "f1": (vec(p["ffn1_norm_pre"]), vec(p["ffn1_norm_post"]), p["ffn1_w_gate"].astype(BF),
               p["ffn1_w_up"].astype(BF), p["ffn1_w_down"].astype(BF)),
        "f2": (vec(p["ffn2_norm_pre"]), vec(p["ffn2_norm_post"]), p["ffn2_w_gate"].astype(BF),
               p["ffn2_w_up"].astype(BF), p["ffn2_w_down"].astype(BF)),
    }


def _gmlp_spatial(ws, bs, rows, n_seq):
    gd = LANES
    if n_seq is None:
        w = ws
        b = jnp.swapaxes(bs, 1, 2)
    else:
        t = rows // n_seq
        eye = jnp.eye(n_seq, dtype=ws.dtype)
        w = jnp.einsum("lgts,bc->lgtbsc", ws[:, :, :t, :t], eye).reshape(ws.shape[0], ws.shape[1], rows, rows)
        b = jnp.repeat(jnp.swapaxes(bs[:, :, :t], 1, 2), n_seq, axis=1)
    return w, jnp.repeat(b, gd, axis=2)


def kernel(x_prompt, x_sample, cache_k, cache_v, cache_idx_k, state_conv, page_table,
           ffn1_norm_pre, ffn1_norm_post, ffn1_w_gate, ffn1_w_up, ffn1_w_down,
           mix_norm_pre, mix_norm_post, w_in, conv_w, conv_b, conv_ln_g, conv_ln_b,
           w_conv_out, w_attn_out, rel_bias, gmlp_ln_g, gmlp_ln_b, gmlp_ws, gmlp_bs,
           w_gmlp_out, w_out, ffn2_norm_pre, ffn2_norm_post, ffn2_w_gate, ffn2_w_up, ffn2_w_down):
    params = dict(ffn1_norm_pre=ffn1_norm_pre, ffn1_norm_post=ffn1_norm_post, ffn1_w_gate=ffn1_w_gate,
                  ffn1_w_up=ffn1_w_up, ffn1_w_down=ffn1_w_down, mix_norm_pre=mix_norm_pre,
                  mix_norm_post=mix_norm_post, w_in=w_in, conv_w=conv_w, conv_b=conv_b,
                  conv_ln_g=conv_ln_g, conv_ln_b=conv_ln_b, w_conv_out=w_conv_out, w_attn_out=w_attn_out,
                  gmlp_ln_g=gmlp_ln_g, gmlp_ln_b=gmlp_ln_b, w_gmlp_out=w_gmlp_out, w_out=w_out,
                  ffn2_norm_pre=ffn2_norm_pre, ffn2_norm_post=ffn2_norm_post, ffn2_w_gate=ffn2_w_gate,
                  ffn2_w_up=ffn2_w_up, ffn2_w_down=ffn2_w_down)
    depth = w_in.shape[0]
    nb, seq, d = x_prompt.shape
    db, t_new, _ = x_sample.shape
    page = cache_k.shape[2]
    n_pages = page_table.shape[1]
    past = n_pages * page
    n_s = db * t_new
    cc = conv_w.shape[2]
    cg = gmlp_ln_g.shape[1]

    w = _prep_weights(params)
    wp = dict(w)
    wp["ws"], wp["bsb"] = _gmlp_spatial(gmlp_ws, gmlp_bs, GMLP_CHUNK, None)
    wsm = dict(w)
    wsm["ws"], wsm["bsb"] = _gmlp_spatial(gmlp_ws, gmlp_bs, n_s, db)

    tq = 256
    tm_p = 512 if seq % 512 == 0 else 256
    k_sel_p = min(TOPK_MAX, seq // 4)
    k_sel_s = min(TOPK_MAX, (past + t_new) // 4)

    r = jnp.arange(tq, dtype=jnp.int32)[None, :]
    c = jnp.arange(2 * tq, dtype=jnp.int32)[:, None]
    band_p = _band_bias(rel_bias, jnp.maximum(tq + r - c, 0))
    band_p = band_p.reshape(N_KV_HEADS, GROUP_SIZE, 2, tq, tq).transpose(0, 2, 3, 1, 4)
    band_p = band_p.reshape(N_KV_HEADS, 2, tq, GROUP_SIZE * tq)
    r8 = jnp.minimum(jnp.arange(8, dtype=jnp.int32), t_new - 1)[:, None]
    c2 = jnp.arange(2 * page, dtype=jnp.int32)[None, :]
    band_s = _band_bias(rel_bias, jnp.maximum(r8 + page - c2, 0)).reshape(N_HEADS * 8, 2 * page)

    ck_t = jnp.transpose(cache_k, (0, 1, 3, 4, 2)).reshape(depth, -1, N_KV_HEADS * HEAD_DIM, page)
    cv_t = jnp.transpose(cache_v, (0, 1, 3, 4, 2)).reshape(depth, -1, N_KV_HEADS * HEAD_DIM, page)
    cki_t = jnp.transpose(cache_idx_k, (0, 1, 3, 2))
    state_t = jnp.swapaxes(state_conv, 1, 2)

    xp = x_prompt.reshape(nb * seq, d)
    xs = jnp.swapaxes(x_sample, 0, 1).reshape(n_s, d)

    def heads_rows(a, width):
        a = a.reshape(t_new, db, -1, width).transpose(1, 2, 0, 3)
        a = jnp.pad(a, ((0, 0), (0, 0), (0, 8 - t_new), (0, 0)), mode="edge")
        return a.reshape(db, -1, width)

    outs = {k: [] for k in ("kp", "vp", "kip", "cp", "ks", "vs", "kis", "cs", "gv")}
    for l in range(depth):
        xp = _ffn(xp, l, *w["f1"], tm=tm_p)
        mi = _mix_in(xp, l, wp, tm=tm_p, n_seq=nb, seq_tiles=seq // tm_p, sample=False)
        att = _attn_prompt(mi, band_p, n_seq=nb, seq=seq, tq=tq, k_sel=k_sel_p)
        xp = _merge_ffn(xp, mi["ca"], att, mi["gm"], mi["gates"], l, w, tm=tm_p)
        kv = mi["kv"].reshape(nb, seq, 2, N_KV_HEADS, HEAD_DIM)
        outs["kp"].append(kv[:, :, 0])
        outs["vp"].append(kv[:, :, 1])
        outs["kip"].append(mi["kw"][:, :IDX_DIM].reshape(nb, seq, IDX_DIM))
        outs["cp"].append(mi["conv_state"])

        xs = _ffn(xs, l, *w["f1"], tm=n_s)
        ms = _mix_in(xs, l, wsm, tm=n_s, n_seq=db, seq_tiles=1, sample=True, state=state_t[l])
        kv_s = jnp.swapaxes(ms["kv"].reshape(t_new, db, 2, N_KV_HEADS * HEAD_DIM), 0, 1)
        ki_s = jnp.swapaxes(ms["kw"][:, :IDX_DIM].reshape(t_new, db, IDX_DIM), 0, 1)
        qi_rows = heads_rows(ms["qi"], IDX_DIM)
        wq_rows = heads_rows(jnp.broadcast_to(
            ms["kw"][:, IDX_DIM:IDX_DIM + N_IDX_HEADS, None], (n_s, N_IDX_HEADS, LANES)
        ).reshape(n_s, N_IDX_HEADS * LANES), LANES)
        qh = heads_rows(ms["q"], HEAD_DIM)
        zero = jnp.zeros_like(qh)
        first = (jnp.arange(N_HEADS * 8) < GROUP_SIZE * 8)[None, :, None]
        q_rows = jnp.concatenate([jnp.where(first, qh, zero), jnp.where(first, zero, qh)], axis=-1)
        new_t = lambda a: jnp.pad(jnp.swapaxes(a, 1, 2), ((0, 0), (0, 0), (0, page - t_new))).astype(BF)
        o = _attn_sample(page_table, qi_rows, wq_rows, q_rows, new_t(ki_s), new_t(kv_s[:, :, 0]),
                         new_t(kv_s[:, :, 1]), band_s, cki_t, ck_t, cv_t, layer=l, k_sel=k_sel_s, n_new=t_new)
        o = o.reshape(db, N_HEADS, 8, N_KV_HEADS, HEAD_DIM)[:, :, :t_new]
        o = jnp.concatenate([o[:, :GROUP_SIZE, :, 0], o[:, GROUP_SIZE:, :, 1]], axis=1)
        att_s = o.transpose(2, 0, 1, 3).reshape(n_s, N_HEADS * HEAD_DIM).astype(BF)
        xs = _merge_ffn(xs, ms["ca"], att_s, ms["gm"], ms["gates"], l, w, tm=n_s)
        outs["ks"].append(kv_s[:, :, 0].reshape(db, t_new, N_KV_HEADS, HEAD_DIM))
        outs["vs"].append(kv_s[:, :, 1].reshape(db, t_new, N_KV_HEADS, HEAD_DIM))
        outs["kis"].append(ki_s)
        outs["cs"].append(jnp.swapaxes(ms["conv_state"], 0, 1))
        outs["gv"].append(jnp.swapaxes(ms["gmlp_v"].reshape(t_new, db, cg), 0, 1))

    yp = xp.reshape(nb, seq, d)
    ys = jnp.swapaxes(xs.reshape(t_new, db, d), 0, 1)
    st = lambda k: jnp.stack(outs[k])
    return (yp, ys, st("kp"), st("vp"), st("kip"), st("cp"),
            st("ks"), st("vs"), st("kis"), st("cs"), st("gv"))
```

```python
import functools
import math

import numpy as np
import jax
import jax.numpy as jnp
from jax import lax
from jax.experimental import pallas as pl
from jax.experimental.pallas import tpu as pltpu

F32 = jnp.float32
BF = jnp.bfloat16

N_HEADS = 8
N_KV_HEADS = 2
HEAD_DIM = 64
GROUP_SIZE = N_HEADS // N_KV_HEADS
N_IDX_HEADS = 8
IDX_DIM = 64
TOPK_MAX = 256
N_BUCKETS = 32
MAX_DISTANCE = 128
N_GMLP_GROUPS = 4
GMLP_CHUNK = 128
EPS = 1e-6

LANES = 128
MASKED = -1e30
M_INIT = -1e29
LOG2E = 1.4426950408889634
F32_LOWEST = -3.4028234663852886e38
VMEM_LIMIT = 56 * 1024 * 1024

NT_DIMS = (((1,), (1,)), ((), ()))


def _cparams(n_axes):
    return pltpu.CompilerParams(dimension_semantics=("arbitrary",) * n_axes,
                                vmem_limit_bytes=VMEM_LIMIT)


def _resident(shape, layer=None):
    nd = len(shape)
    if layer is None:
        return pl.BlockSpec(shape, lambda *_: (0,) * nd, pipeline_mode=pl.Buffered(1))
    return pl.BlockSpec((None,) + tuple(shape), lambda *_: (layer,) + (0,) * nd,
                        pipeline_mode=pl.Buffered(1))


def _rms(x, g):
    return x * lax.rsqrt(jnp.mean(x * x, axis=-1, keepdims=True) + EPS) * g


def _layer_norm(x, g, b):
    mu = jnp.mean(x, axis=-1, keepdims=True)
    xc = x - mu
    return xc * lax.rsqrt(jnp.mean(xc * xc, axis=-1, keepdims=True) + EPS) * g + b


def _dot(a, b):
    return jnp.dot(a, b, preferred_element_type=F32)


def _dot_nt(a, bt):
    return lax.dot_general(a, bt, NT_DIMS, preferred_element_type=F32)


def _sigmoid(x):
    return 0.5 * jnp.tanh(0.5 * x) + 0.5


def _silu(x):
    return x * _sigmoid(x)


FF_CHUNK = 256


def _ffn_half_step(x, pre_ref, post_ref, wg_ref, wu_ref, wd_ref):
    h = _rms(x, pre_ref[...]).astype(BF)
    acc = jnp.zeros(x.shape, F32)
    for c in range(wg_ref.shape[1] // FF_CHUNK):
        sl = slice(c * FF_CHUNK, (c + 1) * FF_CHUNK)
        a = _silu(_dot(h, wg_ref[:, sl])) * _dot(h, wu_ref[:, sl])
        acc = acc + _dot(a.astype(BF), wd_ref[sl, :])
    return x + 0.5 * _rms(acc, post_ref[...])


def _ffn_body(x_ref, pre_ref, post_ref, wg_ref, wu_ref, wd_ref, o_ref):
    o_ref[...] = _ffn_half_step(x_ref[...], pre_ref, post_ref, wg_ref, wu_ref, wd_ref)


def _ffn(x, layer, pre, post, wg, wu, wd, *, tm):
    n, d = x.shape
    dff = wg.shape[2]
    return pl.pallas_call(
        _ffn_body,
        out_shape=jax.ShapeDtypeStruct((n, d), F32),
        grid=(n // tm,),
        in_specs=[pl.BlockSpec((tm, d), lambda i: (i, 0)),
                  _resident((1, d), layer), _resident((1, d), layer),
                  _resident((d, dff), layer), _resident((d, dff), layer), _resident((dff, d), layer)],
        out_specs=pl.BlockSpec((tm, d), lambda i: (i, 0)),
        compiler_params=_cparams(1),
        name="ffn",
    )(x, pre, post, wg, wu, wd)


_MIX_IN = ["x", "pre", "wa", "wrow", "wuv", "wg", "cw", "cb", "clg", "clb", "glg", "glb", "ws", "bsb"]
_Q_SCALE = HEAD_DIM ** -0.5 * LOG2E
_HQ = N_HEADS * HEAD_DIM
_HI = N_IDX_HEADS * IDX_DIM
_HKV = N_KV_HEADS * HEAD_DIM


def _mix_in_body(*refs, names, tm, conv_w, sample, n_seq):
    r = dict(zip(names, refs))
    ca_ref, gm_ref, cs_ref, xin_ref = r["ca"], r["gm"], r["conv_state"], r["xin"]
    cw_ref, cb_ref, clg_ref, clb_ref = r["cw"], r["cb"], r["clg"], r["clb"]
    glg_ref, glb_ref, ws_ref, bsb_ref = r["glg"], r["glb"], r["ws"], r["bsb"]

    h = _rms(r["x"][...], r["pre"][...]).astype(BF)

    def projections():
        row = _dot_nt(h, r["wrow"][...])
        if sample:
            r["q"][...] = (row[:, 0:_HQ] * _Q_SCALE).astype(BF)
            r["qi"][...] = row[:, _HQ:_HQ + _HI].astype(BF)
            o = _HQ + _HI
        else:
            col = _dot_nt(r["wcol"][...], h)
            r["qt"][...] = (col[0:_HQ] * _Q_SCALE).astype(BF)
            r["qit"][...] = col[_HQ:_HQ + _HI].astype(BF)
            vt = col[_HQ + _HI:_HQ + _HI + _HKV].astype(BF)
            tq = r["vt"].shape[2]
            for c in range(tm // tq):
                r["vt"][c] = vt[:, c * tq:(c + 1) * tq]
            r["wit"][...] = col[_HQ + _HI + _HKV:_HQ + _HI + _HKV + N_IDX_HEADS]
            r["ki"][...] = row[:, 2 * _HKV:2 * _HKV + IDX_DIM].astype(BF)
            for g in range(N_KV_HEADS):
                r["kg"][g] = row[:, g * HEAD_DIM:(g + 1) * HEAD_DIM].astype(BF)
            o = 0
        r["kv"][...] = row[:, o:o + 2 * _HKV]
        r["kw"][...] = row[:, o + 2 * _HKV:o + 2 * _HKV + LANES]
        r["gates"][...] = _sigmoid(_dot_nt(h, r["wg"][...])).astype(BF)

    wa_ref, wuv_ref = r["wa"], r["wuv"]

    a = _dot_nt(h, wa_ref[...])
    uv = jax.nn.gelu(_dot_nt(h, wuv_ref[...]))
    cc = a.shape[1] // 2
    glu = a[:, :cc] * _sigmoid(a[:, cc:])
    pre_rows = conv_w - 1
    if sample:
        n_t = tm // n_seq
        xin_ref[0:pre_rows] = r["state"][...]
        for t in range(n_t):
            xin_ref[pre_rows + t] = glu[t * n_seq:(t + 1) * n_seq, :]
        ys = []
        for t in range(n_t):
            y = jnp.zeros((n_seq, cc), F32)
            for j in range(conv_w):
                y = y + xin_ref[t + j] * cw_ref[j:j + 1, :]
            ys.append(y)
        y = jnp.concatenate(ys, axis=0)
        cs_ref[...] = xin_ref[n_t:n_t + pre_rows]
        ca_ref[...] = _silu(_layer_norm(y + cb_ref[...], clg_ref[...], clb_ref[...])).astype(BF)
    else:
        halo = 32
        @pl.when(pl.program_id(1) == 0)
        def _():
            xin_ref[0:halo, :] = jnp.zeros((halo, cc), F32)
        xin_ref[halo:halo + tm, :] = glu
        rb = 64
        off = halo - pre_rows
        sh_ref = r["shifted"]
        for b in range(8):
            n_rows = tm + 8 * (len(range(b, conv_w, 8)) - 1)
            sh_ref[b, 0:n_rows, :] = xin_ref[off + b:off + b + n_rows, :]
        cs_ref[...] = xin_ref[halo + tm - pre_rows:halo + tm, :]
        xin_ref[0:halo, :] = xin_ref[tm:tm + halo, :]

    cg = uv.shape[1] // 2
    u = uv[:, :cg]
    vn = _layer_norm(uv[:, cg:], glg_ref[...], glb_ref[...])
    if sample:
        r["gmlp_v"][...] = vn
    vb = vn.astype(BF)

    projections()

    if not sample:
        for r0 in range(0, tm, rb):
            y = jnp.zeros((rb, cc), F32)
            for j in range(conv_w):
                y = y + sh_ref[j % 8, r0 + j - j % 8:r0 + j - j % 8 + rb, :] * cw_ref[j:j + 1, :]
            y = _silu(_layer_norm(y + cb_ref[...], clg_ref[...], clb_ref[...]))
            ca_ref[r0:r0 + rb, :] = y.astype(BF)

    gd = cg // N_GMLP_GROUPS
    chunk = ws_ref.shape[-1]
    ri = lax.broadcasted_iota(jnp.int32, (chunk, chunk), 0)
    ci = lax.broadcasted_iota(jnp.int32, (chunk, chunk), 1)
    wsm = [jnp.where(ci <= ri, ws_ref[g], 0.0).astype(BF) for g in range(N_GMLP_GROUPS)]
    for n in range(tm // chunk):
        rs = slice(n * chunk, (n + 1) * chunk)
        mixed = jnp.concatenate(
            [_dot(wsm[g], vb[rs, g * gd:(g + 1) * gd]) for g in range(N_GMLP_GROUPS)], axis=1)
        gm_ref[rs, :] = (u[rs, :] * (mixed + bsb_ref[...])).astype(BF)


def _mix_in(x, layer, w, *, tm, n_seq, seq_tiles, sample, state=None):
    n, d = x.shape
    conv_w = w["conv_w"].shape[1]
    cc = w["conv_w"].shape[2]
    cg = w["gmlp_ln_g"].shape[2]
    pre_rows = conv_w - 1
    if sample:
        grid = (1,)
        row = lambda i: (0, 0)
        res = lambda shape: _resident(shape, layer)
        cs_shape = (pre_rows, n_seq, cc)
        cs_spec = pl.BlockSpec(cs_shape, lambda i: (0, 0, 0))
        scratch = {"xin": pltpu.VMEM((pre_rows + tm // n_seq, n_seq, cc), F32)}
    else:
        grid = (n_seq, seq_tiles)
        row = lambda b, t: (b * seq_tiles + t, 0)
        res = lambda shape: _resident(shape, layer)
        cs_shape = (n_seq, pre_rows, cc)
        cs_spec = pl.BlockSpec((None, pre_rows, cc), lambda b, t: (b, 0, 0))
        scratch = {"xin": pltpu.VMEM((32 + tm, cc), F32),
                   "shifted": pltpu.VMEM((8, tm + 8 * ((conv_w - 1) // 8), cc), F32)}

    wrow = w["wrow_s"] if sample else w["wrow_p"]
    in_names = list(_MIX_IN)
    args = [x, w["mix_pre"], w["wa"], wrow, w["wuv"], w["wg"], w["conv_w"], w["conv_b"], w["conv_ln_g"],
            w["conv_ln_b"], w["gmlp_ln_g"], w["gmlp_ln_b"], w["ws"], w["bsb"]]
    in_specs = [pl.BlockSpec((tm, d), row)] + [res(a.shape[1:]) for a in args[1:]]
    if sample:
        in_names.append("state")
        in_specs.append(pl.BlockSpec(cs_shape, lambda i: (0, 0, 0)))
        args.append(state)
    else:
        in_names.append("wcol")
        in_specs.append(res(w["wcol"].shape[1:]))
        args.append(w["wcol"])

    rows = lambda wd, dt: ((n, wd), (tm, wd), row, dt)
    outs = {"kv": rows(2 * _HKV, F32), "kw": rows(LANES, F32), "ca": rows(cc, BF), "gm": rows(cg, BF),
            "gates": rows(w["wg"].shape[1], BF), "conv_state": (cs_shape, cs_spec.block_shape, cs_spec.index_map, F32)}
    if sample:
        outs.update(q=rows(_HQ, BF), qi=rows(_HI, BF), gmlp_v=rows(cg, F32))
    else:
        tq = 256
        cols = lambda ht, dt: ((ht, n), (ht, tm), lambda b, t: (0, b * seq_tiles + t), dt)
        outs.update(qt=cols(_HQ, BF), qit=cols(_HI, BF), wit=cols(N_IDX_HEADS, F32), ki=rows(IDX_DIM, BF),
                    kg=((N_KV_HEADS, n, HEAD_DIM), (N_KV_HEADS, tm, HEAD_DIM),
                        lambda b, t: (0, b * seq_tiles + t, 0), BF),
                    vt=((n // tq, _HKV, tq), (tm // tq, _HKV, tq), lambda b, t: (b * seq_tiles + t, 0, 0), BF))
    out_names = list(outs)

    res_out = pl.pallas_call(
        functools.partial(_mix_in_body, names=in_names + out_names + list(scratch), tm=tm, conv_w=conv_w,
                          sample=sample, n_seq=n_seq),
        out_shape=[jax.ShapeDtypeStruct(outs[k][0], outs[k][3]) for k in out_names],
        grid=grid, in_specs=in_specs,
        out_specs=[pl.BlockSpec(outs[k][1], outs[k][2]) for k in out_names],
        scratch_shapes=list(scratch.values()), compiler_params=_cparams(len(grid)),
        name="mix_in_sample" if sample else "mix_in",
    )(*args)
    return dict(zip(out_names, res_out))


def _ordered_bits_to_float(u):
    t = u ^ jnp.int32(-2 ** 31)
    fb = t ^ (lax.shift_right_arithmetic(t, 31) & jnp.int32(0x7FFFFFFF))
    return lax.bitcast_convert_type(fb, F32)


def _row_total(cnt, ones_bf):
    return _dot(cnt.astype(BF), ones_bf)


def _select_topk(sc_ref, nk, rows, width, k_sel):
    nh = width // LANES
    ones_bf = jnp.ones((LANES, LANES), BF)
    kf = float(k_sel)

    def halves(x):
        return [x[:, i * LANES:(i + 1) * LANES] for i in range(nh)]

    def count(pred):
        def body(c, cnt):
            for xh in halves(sc_ref[c]):
                cnt = cnt + jnp.where(pred(xh), 1.0, 0.0)
            return cnt
        return _row_total(lax.fori_loop(0, nk, body, jnp.zeros((rows, LANES), F32)), ones_bf)

    def bit_body(i, u):
        cand = u | jnp.left_shift(jnp.int32(1), 31 - i)
        thr = _ordered_bits_to_float(cand)
        return jnp.where(count(lambda xh: xh >= thr) >= kf, cand, u)

    u = lax.fori_loop(0, 32, bit_body, jnp.zeros((rows, LANES), jnp.int32))
    thr = _ordered_bits_to_float(u)
    thr = jnp.where(thr >= F32_LOWEST, thr, F32_LOWEST)

    n_ge = count(lambda xh: xh >= thr)

    @pl.when(jnp.max(n_ge) > kf)
    def _():
        need = kf - count(lambda xh: xh > thr)
        ri = lax.broadcasted_iota(jnp.int32, (LANES, LANES), 0)
        ci = lax.broadcasted_iota(jnp.int32, (LANES, LANES), 1)
        before = jnp.where(ri < ci, 1.0, 0.0).astype(BF)

        def fix(c, seen):
            out = []
            for xh in halves(sc_ref[c]):
                tie = xh == thr
                tf = jnp.where(tie, 1.0, 0.0).astype(BF)
                rank = seen + _dot(tf, before)
                out.append(jnp.where(jnp.where(tie, rank, -1.0) >= need, -jnp.inf, xh))
                seen = seen + _dot(tf, ones_bf)
            sc_ref[c] = out[0] if nh == 1 else jnp.concatenate(out, axis=1)
            return seen

        lax.fori_loop(0, nk, fix, jnp.zeros((rows, LANES), F32))

    def to_mask(c, carry):
        x = sc_ref[c]
        thr_w = thr if nh == 1 else jnp.concatenate([thr] * nh, axis=1)
        sc_ref[c] = jnp.where(x >= thr_w, 0.0, MASKED)
        return carry

    lax.fori_loop(0, nk, to_mask, 0)


def _select_topk_cols(sc_ref, hi_ref, lo_ref, nk, n_keys, n_q, k_sel):
    kf = float(k_sel)
    groups = n_keys // 8
    slabs = n_keys // 16
    i16 = jnp.int16
    low_mask = jnp.int32(0xFFFF)
    bias16 = jnp.int32(1 << 15)

    def split(c, carry):
        bits = lax.bitcast_convert_type(sc_ref[c], jnp.int32)
        key = bits ^ (lax.shift_right_arithmetic(bits, 31) & jnp.int32(0x7FFFFFFF))
        hi_ref[c] = lax.shift_right_arithmetic(key, 16).astype(i16)
        lo_ref[c] = ((key & low_mask) - bias16).astype(i16)
        return carry

    lax.fori_loop(0, nk, split, 0)

    def count16(ref, pred):
        def body(c, acc):
            for s in range(slabs):
                acc = acc + jnp.where(pred(ref[c, s * 16:(s + 1) * 16, :]), i16(1), i16(0))
            return acc
        acc = lax.fori_loop(0, nk, body, jnp.zeros((16, n_q), i16))
        return jnp.sum(acc.astype(F32), axis=0, keepdims=True)

    def tile16(v):
        return jnp.broadcast_to(v, (16, n_q)).astype(i16)

    def search16(ref, need):
        def bit_body(i, u):
            cand = u | jnp.left_shift(jnp.int32(1), 15 - i)
            c16 = tile16(cand - bias16)
            return jnp.where(count16(ref, lambda x: x >= c16) >= need, cand, u)
        return lax.fori_loop(0, 16, bit_body, jnp.zeros((1, n_q), jnp.int32)) - bias16

    hi_k = search16(hi_ref, kf)
    hi16 = tile16(hi_k)
    need_lo = kf - count16(hi_ref, lambda x: x > hi16)

    def keep_boundary(c, carry):
        for s in range(slabs):
            rows = slice(s * 16, (s + 1) * 16)
            lo_ref[c, rows, :] = jnp.where(hi_ref[c, rows, :] == hi16, lo_ref[c, rows, :], i16(-(1 << 15)))
        return carry

    lax.fori_loop(0, nk, keep_boundary, 0)
    lo_k = search16(lo_ref, need_lo)
    key_k = jnp.left_shift(hi_k, 16) | (lo_k + bias16)
    thr = lax.bitcast_convert_type(
        key_k ^ (lax.shift_right_arithmetic(key_k, 31) & jnp.int32(0x7FFFFFFF)), F32)
    thr = jnp.where(thr >= F32_LOWEST, thr, F32_LOWEST)

    def count(pred):
        def body(c, cnt):
            x = sc_ref[c].reshape(groups, 8, n_q)
            return cnt + jnp.sum(jnp.where(pred(x), 1.0, 0.0), axis=0)
        cnt = lax.fori_loop(0, nk, body, jnp.zeros((8, n_q), F32))
        return jnp.sum(cnt, axis=0, keepdims=True)

    thr8 = jnp.broadcast_to(thr, (8, n_q))[None]

    @pl.when(jnp.max(count(lambda x: x >= thr8)) > kf)
    def _():
        need = kf - count(lambda x: x > thr8)
        ri = lax.broadcasted_iota(jnp.int32, (n_keys, n_keys), 0)
        ci = lax.broadcasted_iota(jnp.int32, (n_keys, n_keys), 1)
        before = jnp.where(ci < ri, 1.0, 0.0).astype(BF)

        def fix(c, seen):
            x = sc_ref[c]
            tie = x == thr
            tf = jnp.where(tie, 1.0, 0.0)
            rank = seen + _dot(before, tf.astype(BF))
            sc_ref[c] = jnp.where(jnp.where(tie, rank, -1.0) >= need, -jnp.inf, x)
            return seen + jnp.sum(tf, axis=0, keepdims=True)

        lax.fori_loop(0, nk, fix, jnp.zeros((1, n_q), F32))

    def to_mask(c, carry):
        sc_ref[c] = jnp.where(sc_ref[c] >= thr, 0.0, MASKED)
        return carry

    lax.fori_loop(0, nk, to_mask, 0)


def _attn_prompt_body(qt_ref, qit_ref, wit_ref, ki_ref, kg_ref, vt_ref, band_ref, o_ref, sc_ref, hi_ref, lo_ref,
                      *, tq, k_sel, heads_per_stream):
    j = pl.program_id(1)
    nk = j + 1

    def heads_along_lanes(x, heads, width):
        return jnp.concatenate([x[h * width:(h + 1) * width, :] for h in heads], axis=1)

    qi_all = heads_along_lanes(qit_ref[...], range(N_IDX_HEADS), IDX_DIM)
    w_all = heads_along_lanes(wit_ref[...], range(N_IDX_HEADS), 1)
    halves = tq // LANES
    q_pos = lax.broadcasted_iota(jnp.int32, (tq, LANES), 1) + j * tq

    def score_chunk(c, carry):
        k0 = pl.multiple_of(c * tq, tq)
        kic = ki_ref[pl.ds(k0, tq), :]
        k_pos = lax.broadcasted_iota(jnp.int32, (tq, LANES), 0) + k0
        for i in range(halves):
            acc = None
            for h in range(N_IDX_HEADS):
                ls = slice(h * tq + i * LANES, h * tq + (i + 1) * LANES)
                term = jnp.maximum(_dot(kic, qi_all[:, ls]), 0.0) * w_all[:, ls]
                acc = term if acc is None else acc + term
            sc_ref[c, :, i * LANES:(i + 1) * LANES] = jnp.where(k_pos <= q_pos + i * LANES, acc, -jnp.inf)
        return carry

    lax.fori_loop(0, nk, score_chunk, 0)
    _select_topk_cols(sc_ref, hi_ref, lo_ref, nk, tq, tq, k_sel)

    qt = qt_ref[...]
    n_streams = N_HEADS // heads_per_stream
    per_group = GROUP_SIZE // heads_per_stream
    wide = heads_per_stream * tq
    q_s = [heads_along_lanes(qt, range(i * heads_per_stream, (i + 1) * heads_per_stream), HEAD_DIM)
           for i in range(n_streams)]

    def chunk_step(c, carry, bias):
        k0 = pl.multiple_of(c * tq, tq)
        mask = jnp.concatenate([sc_ref[c]] * heads_per_stream, axis=1)
        logits = []
        for i in range(n_streams):
            g = i // per_group
            s = _dot(kg_ref[g, pl.ds(k0, tq), :], q_s[i]) + mask
            if bias is not None:
                s = s + bias(g, slice((i % per_group) * wide, (i % per_group + 1) * wide))
            logits.append(s)
        stats = []
        for i in range(n_streams):
            m, l, _ = carry[i]
            m_new = jnp.maximum(m, jnp.max(logits[i], axis=0, keepdims=True))
            alpha = jnp.exp2(m - m_new)
            p = jnp.exp2(logits[i] - m_new)
            stats.append((m_new, alpha, alpha * l + jnp.sum(p, axis=0, keepdims=True), p.astype(BF)))
        out = []
        for i in range(n_streams):
            g = i // per_group
            m_new, alpha, l_new, p = stats[i]
            vt = vt_ref[c, g * HEAD_DIM:(g + 1) * HEAD_DIM, :]
            out.append((m_new, l_new, alpha * carry[i][2] + _dot(vt, p)))
        return out

    carry = [(jnp.full((1, wide), M_INIT, F32), jnp.zeros((1, wide), F32), jnp.zeros((HEAD_DIM, wide), F32))
             for _ in range(n_streams)]
    j_prev = jnp.maximum(j - 1, 0)
    prev_mask = jnp.where(j > 0, 0.0, MASKED)
    carry = lax.fori_loop(0, j_prev, lambda c, carry: chunk_step(c, carry, None), carry)
    carry = chunk_step(j_prev, carry, lambda g, ls: band_ref[g, 0, :, ls] + prev_mask)
    carry = chunk_step(j, carry, lambda g, ls: band_ref[g, 1, :, ls])
    o_t = []
    for m, l, acc in carry:
        o_s = acc / l
        o_t += [o_s[:, hh * tq:(hh + 1) * tq] for hh in range(heads_per_stream)]
    o_ref[...] = jnp.concatenate(o_t, axis=0).T.astype(BF)


def _attn_prompt(mi, band, *, n_seq, seq, tq, k_sel):
    nt = seq // tq
    col = lambda b, t: (0, b * nt + t)
    hd = N_HEADS * HEAD_DIM
    return pl.pallas_call(
        functools.partial(_attn_prompt_body, tq=tq, k_sel=k_sel, heads_per_stream=2),
        out_shape=jax.ShapeDtypeStruct((n_seq * seq, hd), BF),
        grid=(n_seq, nt),
        in_specs=[pl.BlockSpec((hd, tq), col), pl.BlockSpec((N_IDX_HEADS * IDX_DIM, tq), col),
                  pl.BlockSpec((N_IDX_HEADS, tq), col),
                  pl.BlockSpec((seq, IDX_DIM), lambda b, t: (b, 0)),
                  pl.BlockSpec((N_KV_HEADS, seq, HEAD_DIM), lambda b, t: (0, b, 0)),
                  pl.BlockSpec((nt, N_KV_HEADS * HEAD_DIM, tq), lambda b, t: (b, 0, 0)),
                  _resident(band.shape)],
        out_specs=pl.BlockSpec((tq, hd), lambda b, t: (b * nt + t, 0)),
        scratch_shapes=[pltpu.VMEM((nt, tq, tq), F32), pltpu.VMEM((nt, tq, tq), jnp.int16),
                        pltpu.VMEM((nt, tq, tq), jnp.int16)],
        compiler_params=_cparams(2),
        name="attn_prompt",
    )(mi["qt"], mi["qit"], mi["wit"], mi["ki"], mi["kg"], mi["vt"], band)


def _page_stream(cache_ref, buf_ref, sem_ref, pt_ref, layer, n_pages, page):
    def copy(seq, slot, p):
        return pltpu.make_async_copy(cache_ref.at[layer, pt_ref[seq, p]],
                                     buf_ref.at[slot, :, p * page:(p + 1) * page], sem_ref.at[slot])

    def start(seq, slot):
        for p in range(n_pages):
            copy(seq, slot, p).start()

    def wait(seq, slot):
        for p in range(n_pages):
            copy(seq, slot, p).wait()

    return start, wait


def _double_buffered(streams):
    i = pl.program_id(0)
    slot = i % 2

    @pl.when(i == 0)
    def _():
        for start, _ in streams:
            start(0, 0)

    @pl.when(i + 1 < pl.num_programs(0))
    def _():
        for start, _ in streams:
            start(i + 1, 1 - slot)

    for _, wait in streams:
        wait(i, slot)
    return slot


def _idx_scores_body(pt_ref, qi_ref, wq_ref, kin_ref, cki_ref, o_ref, buf_ref, sem_ref,
                     *, layer, n_pages, page, chunk):
    slot = _double_buffered([_page_stream(cki_ref, buf_ref, sem_ref, pt_ref, layer, n_pages, page)])
    qi = qi_ref[...]
    wq = wq_ref[...]
    rows_t = o_ref.shape[0]
    past = n_pages * page

    def head_sum(s):
        w = s.shape[1]
        wgt = wq if w == LANES else jnp.concatenate([wq] * (w // LANES), axis=1)
        s = (jnp.maximum(s, 0.0) * wgt).reshape(N_IDX_HEADS, rows_t, w)
        acc = s[0]
        for h in range(1, N_IDX_HEADS):
            acc = acc + s[h]
        return acc

    for c in range(past // chunk):
        cs = slice(c * chunk, (c + 1) * chunk)
        o_ref[:, cs] = head_sum(_dot(qi, buf_ref[slot, :, cs].astype(BF)))
    t = lax.broadcasted_iota(jnp.int32, (rows_t, page), 0)
    cc = lax.broadcasted_iota(jnp.int32, (rows_t, page), 1)
    o_ref[:, past:past + page] = jnp.where(cc <= t, head_sum(_dot(qi, kin_ref[...])), -jnp.inf)


def _select_body(sc_ref, o_ref, *, k_sel):
    o_ref[...] = sc_ref[...]
    _select_topk(o_ref, 1, o_ref.shape[1], o_ref.shape[2], k_sel)


def _attend_sample_body(pt_ref, q_ref, mask_ref, band_ref, kn_ref, vn_ref, ck_ref, cv_ref, o_ref,
                        kbuf_ref, vbuf_ref, s_ref, ksem_ref, vsem_ref, *, layer, n_pages, page, chunk):
    slot = _double_buffered([_page_stream(ck_ref, kbuf_ref, ksem_ref, pt_ref, layer, n_pages, page),
                             _page_stream(cv_ref, vbuf_ref, vsem_ref, pt_ref, layer, n_pages, page)])
    q = q_ref[...]
    rows = q.shape[0]
    rows_t = mask_ref.shape[0]
    past = n_pages * page
    n_chunks = past // chunk

    def add_mask(s, m):
        w = s.shape[1]
        return (s.reshape(N_HEADS, rows_t, w) + m[None]).reshape(rows, w)

    def lane_fold(x, fn, init):
        for i in range(x.shape[1] // LANES):
            init = fn(init, x[:, i * LANES:(i + 1) * LANES])
        return init

    mx = jnp.full((rows, LANES), M_INIT, F32)
    for c in range(n_chunks):
        cs = slice(c * chunk, (c + 1) * chunk)
        s = add_mask(_dot(q, kbuf_ref[slot, :, cs].astype(BF)), mask_ref[:, cs])
        if c == n_chunks - 1:
            s = jnp.concatenate([s[:, :chunk - page], s[:, chunk - page:] + band_ref[:, 0:page]], axis=1)
        s_ref[:, cs] = s
        mx = lane_fold(s, jnp.maximum, mx)
    s_new = add_mask(_dot(q, kn_ref[...]), mask_ref[:, past:past + page]) + band_ref[:, page:2 * page]
    m = jnp.max(jnp.maximum(mx, s_new), axis=1, keepdims=True)

    p_new = jnp.exp2(s_new - m)
    l = p_new
    acc = lax.dot_general(p_new.astype(BF), vn_ref[...], NT_DIMS, preferred_element_type=F32)
    for c in range(n_chunks):
        cs = slice(c * chunk, (c + 1) * chunk)
        p = jnp.exp2(s_ref[:, cs] - m)
        l = lane_fold(p, jnp.add, l)
        acc = acc + lax.dot_general(p.astype(BF), vbuf_ref[slot, :, cs].astype(BF), NT_DIMS,
                                    preferred_element_type=F32)
    o_ref[...] = acc / jnp.sum(l, axis=1, keepdims=True)


def _attn_sample(page_table, qi_rows, wq_rows, q_rows, ki_new_t, k_new_t, v_new_t, band, cki_t, ck_t, cv_t,
                 *, layer, k_sel, n_new):
    n_seq, n_pages = page_table.shape
    page = ck_t.shape[3]
    rows = q_rows.shape[1]
    rows_t = rows // N_HEADS
    past = n_pages * page
    width = past + page
    chunk = 1024 if past % 1024 == 0 else page
    per_seq3 = lambda b, pt: (b, 0, 0)
    any_spec = pl.BlockSpec(memory_space=pl.ANY)

    scores = pl.pallas_call(
        functools.partial(_idx_scores_body, layer=layer, n_pages=n_pages, page=page, chunk=chunk),
        out_shape=jax.ShapeDtypeStruct((n_seq, rows_t, width), F32),
        grid_spec=pltpu.PrefetchScalarGridSpec(
            num_scalar_prefetch=1, grid=(n_seq,),
            in_specs=[pl.BlockSpec((None, rows, IDX_DIM), per_seq3),
                      pl.BlockSpec((None, rows, LANES), per_seq3),
                      pl.BlockSpec((None, IDX_DIM, page), per_seq3),
                      any_spec],
            out_specs=pl.BlockSpec((None, rows_t, width), per_seq3),
            scratch_shapes=[pltpu.VMEM((2, IDX_DIM, past), F32), pltpu.SemaphoreType.DMA((2,))]),
        compiler_params=_cparams(1),
        name="idx_scores_sample",
    )(page_table, qi_rows, wq_rows, ki_new_t, cki_t)

    n_sel = n_seq * n_new
    sel_rows = 64 if n_sel % 64 == 0 else n_sel
    mask = pl.pallas_call(
        functools.partial(_select_body, k_sel=k_sel),
        out_shape=jax.ShapeDtypeStruct((1, n_sel, width), F32),
        grid=(n_sel // sel_rows,),
        in_specs=[pl.BlockSpec((1, sel_rows, width), lambda i: (0, i, 0))],
        out_specs=pl.BlockSpec((1, sel_rows, width), lambda i: (0, i, 0)),
        compiler_params=_cparams(1),
        name="select_sample",
    )(scores[:, :n_new].reshape(1, n_sel, width)).reshape(n_seq, n_new, width)
    mask = jnp.pad(mask, ((0, 0), (0, rows_t - n_new), (0, 0)), mode="edge")

    return pl.pallas_call(
        functools.partial(_attend_sample_body, layer=layer, n_pages=n_pages, page=page, chunk=chunk),
        out_shape=jax.ShapeDtypeStruct((n_seq, rows, LANES), F32),
        grid_spec=pltpu.PrefetchScalarGridSpec(
            num_scalar_prefetch=1, grid=(n_seq,),
            in_specs=[pl.BlockSpec((None, rows, LANES), per_seq3),
                      pl.BlockSpec((None, rows_t, width), per_seq3),
                      pl.BlockSpec(band.shape, lambda b, pt: (0, 0), pipeline_mode=pl.Buffered(1)),
                      pl.BlockSpec((None, LANES, page), per_seq3),
                      pl.BlockSpec((None, LANES, page), per_seq3),
                      any_spec, any_spec],
            out_specs=pl.BlockSpec((None, rows, LANES), per_seq3),
            scratch_shapes=[pltpu.VMEM((2, LANES, past), F32), pltpu.VMEM((2, LANES, past), F32),
                            pltpu.VMEM((rows, past), F32),
                            pltpu.SemaphoreType.DMA((2,)), pltpu.SemaphoreType.DMA((2,))]),
        compiler_params=_cparams(1),
        name="attend_sample",
    )(page_table, q_rows, mask, band, k_new_t, v_new_t, ck_t, cv_t)


def _merge_ffn_body(x_ref, ca_ref, att_ref, gm_ref, gt_ref, wco_ref, wao_ref, wgo_ref, wout_ref, post_ref,
                    fpre_ref, fpost_ref, wg_ref, wu_ref, wd_ref, o_ref):
    d = x_ref.shape[1]
    gt = gt_ref[...].astype(F32)
    m = (gt[:, 0:d] * _dot(ca_ref[...], wco_ref[...])
         + gt[:, d:2 * d] * _dot(att_ref[...], wao_ref[...])
         + gt[:, 2 * d:3 * d] * _dot(gm_ref[...], wgo_ref[...]))
    x = x_ref[...] + _rms(_dot(m.astype(BF), wout_ref[...]), post_ref[...])
    o_ref[...] = _ffn_half_step(x, fpre_ref, fpost_ref, wg_ref, wu_ref, wd_ref)


def _merge_ffn(x, ca, att, gm, gates, layer, w, *, tm):
    n, d = x.shape
    row = lambda i: (i, 0)
    weights = [w["wco"], w["wao"], w["wgo"], w["wout"], w["mix_post"], *w["f2"]]
    return pl.pallas_call(
        _merge_ffn_body,
        out_shape=jax.ShapeDtypeStruct((n, d), F32),
        grid=(n // tm,),
        in_specs=[pl.BlockSpec((tm, d), row), pl.BlockSpec((tm, ca.shape[1]), row),
                  pl.BlockSpec((tm, att.shape[1]), row), pl.BlockSpec((tm, gm.shape[1]), row),
                  pl.BlockSpec((tm, gates.shape[1]), row)] + [_resident(a.shape[1:], layer) for a in weights],
        out_specs=pl.BlockSpec((tm, d), row),
        compiler_params=_cparams(1),
        name="merge_ffn",
    )(x, ca, att, gm, gates, *weights)


def _band_body(rb_ref, bucket_ref, o_ref):
    h = pl.program_id(0)
    bucket = bucket_ref[...]
    far = rb_ref[N_BUCKETS - 1, h]
    tile = jnp.zeros(bucket.shape, F32)
    for bkt in range(N_BUCKETS):
        tile = jnp.where(bucket == bkt, rb_ref[bkt, h] - far, tile)
    o_ref[...] = tile * LOG2E


def _band_bias(rel_bias, dist):
    max_exact = N_BUCKETS // 2
    d = jnp.maximum(dist, 1).astype(F32)
    large = max_exact + (jnp.log(d / max_exact) / math.log(MAX_DISTANCE / max_exact)
                         * (N_BUCKETS - max_exact)).astype(jnp.int32)
    bucket = jnp.where(dist < max_exact, dist, jnp.minimum(large, N_BUCKETS - 1))
    r, c = dist.shape
    return pl.pallas_call(
        _band_body,
        out_shape=jax.ShapeDtypeStruct((N_HEADS, r, c), F32),
        grid=(N_HEADS,),
        in_specs=[pl.BlockSpec(memory_space=pltpu.SMEM), pl.BlockSpec((r, c), lambda h: (0, 0))],
        out_specs=pl.BlockSpec((None, r, c), lambda h: (h, 0, 0)),
        compiler_params=_cparams(1),
        name="band_bias",
    )(rel_bias, bucket)


def _prep_weights(p):
    d = p["w_in"].shape[1]
    cc = p["conv_w"].shape[2]
    cg = p["gmlp_ln_g"].shape[1]
    hq = N_HEADS * HEAD_DIM
    hk = N_KV_HEADS * HEAD_DIM
    hi = N_IDX_HEADS * IDX_DIM
    win_t = jnp.swapaxes(p["w_in"], 1, 2).astype(BF)
    o = 0
    cols = {}
    for name, wd in (("a", 2 * cc), ("q", hq), ("k", hk), ("v", hk), ("qi", hi), ("ki", IDX_DIM),
                     ("wi", N_IDX_HEADS), ("uv", 2 * cg), ("g", 3 * d)):
        cols[name] = win_t[:, o:o + wd, :]
        o += wd
    pad = jnp.zeros((win_t.shape[0], LANES - IDX_DIM - N_IDX_HEADS, d), BF)
    vec = lambda a: a[:, None, :]
    kvw = [cols["k"], cols["v"], cols["ki"], cols["wi"], pad]
    return {
        "wa": cols["a"],
        "wrow_p": jnp.concatenate(kvw, axis=1),
        "wrow_s": jnp.concatenate([cols["q"], cols["qi"]] + kvw, axis=1),
        "wcol": jnp.concatenate([cols["q"], cols["qi"], cols["v"], cols["wi"]], axis=1),
        "wuv": cols["uv"], "wg": cols["g"],
        "mix_pre": vec(p["mix_norm_pre"]), "mix_post": vec(p["mix_norm_post"]),
        "conv_w": p["conv_w"], "conv_b": vec(p["conv_b"]),
        "conv_ln_g": vec(p["conv_ln_g"]), "conv_ln_b": vec(p["conv_ln_b"]),
        "gmlp_ln_g": vec(p["gmlp_ln_g"]), "gmlp_ln_b": vec(p["gmlp_ln_b"]),
        "wco": p["w_conv_out"].astype(BF), "wao": p["w_attn_out"].astype(BF),
        "wgo": p["w_gmlp_out"].astype(BF), "wout": p["w_out"].astype(BF),
        "f1": (vec(p["ffn1_norm_pre"]), vec(p["ffn1_norm_post"]), p["ffn1_w_gate"].astype(BF),
               p["ffn1_w_up"].astype(BF), p["ffn1_w_down"].astype(BF)),
        "f2": (vec(p["ffn2_norm_pre"]), vec(p["ffn2_norm_post"]), p["ffn2_w_gate"].astype(BF),
               p["ffn2_w_up"].astype(BF), p["ffn2_w_down"].astype(BF)),
    }


def _gmlp_spatial(ws, bs, rows, n_seq):
    gd = LANES
    if n_seq is None:
        w = ws
        b = jnp.swapaxes(bs, 1, 2)
    else:
        t = rows // n_seq
        eye = jnp.eye(n_seq, dtype=ws.dtype)
        w = jnp.einsum("lgts,bc->lgtbsc", ws[:, :, :t, :t], eye).reshape(ws.shape[0], ws.shape[1], rows, rows)
        b = jnp.repeat(jnp.swapaxes(bs[:, :, :t], 1, 2), n_seq, axis=1)
    return w, jnp.repeat(b, gd, axis=2)


def kernel(x_prompt, x_sample, cache_k, cache_v, cache_idx_k, state_conv, page_table,
           ffn1_norm_pre, ffn1_norm_post, ffn1_w_gate, ffn1_w_up, ffn1_w_down,
           mix_norm_pre, mix_norm_post, w_in, conv_w, conv_b, conv_ln_g, conv_ln_b,
           w_conv_out, w_attn_out, rel_bias, gmlp_ln_g, gmlp_ln_b, gmlp_ws, gmlp_bs,
           w_gmlp_out, w_out, ffn2_norm_pre, ffn2_norm_post, ffn2_w_gate, ffn2_w_up, ffn2_w_down):
    params = dict(ffn1_norm_pre=ffn1_norm_pre, ffn1_norm_post=ffn1_norm_post, ffn1_w_gate=ffn1_w_gate,
                  ffn1_w_up=ffn1_w_up, ffn1_w_down=ffn1_w_down, mix_norm_pre=mix_norm_pre,
                  mix_norm_post=mix_norm_post, w_in=w_in, conv_w=conv_w, conv_b=conv_b,
                  conv_ln_g=conv_ln_g, conv_ln_b=conv_ln_b, w_conv_out=w_conv_out, w_attn_out=w_attn_out,
                  gmlp_ln_g=gmlp_ln_g, gmlp_ln_b=gmlp_ln_b, w_gmlp_out=w_gmlp_out, w_out=w_out,
                  ffn2_norm_pre=ffn2_norm_pre, ffn2_norm_post=ffn2_norm_post, ffn2_w_gate=ffn2_w_gate,
                  ffn2_w_up=ffn2_w_up, ffn2_w_down=ffn2_w_down)
    depth = w_in.shape[0]
    nb, seq, d = x_prompt.shape
    db, t_new, _ = x_sample.shape
    page = cache_k.shape[2]
    n_pages = page_table.shape[1]
    past = n_pages * page
    n_s = db * t_new
    cc = conv_w.shape[2]
    cg = gmlp_ln_g.shape[1]

    w = _prep_weights(params)
    wp = dict(w)
    wp["ws"], wp["bsb"] = _gmlp_spatial(gmlp_ws, gmlp_bs, GMLP_CHUNK, None)
    wsm = dict(w)
    wsm["ws"], wsm["bsb"] = _gmlp_spatial(gmlp_ws, gmlp_bs, n_s, db)

    tq = 256
    tm_p = 512 if seq % 512 == 0 else 256
    k_sel_p = min(TOPK_MAX, seq // 4)
    k_sel_s = min(TOPK_MAX, (past + t_new) // 4)

    r = jnp.arange(tq, dtype=jnp.int32)[None, :]
    c = jnp.arange(2 * tq, dtype=jnp.int32)[:, None]
    band_p = _band_bias(rel_bias, jnp.maximum(tq + r - c, 0))
    band_p = band_p.reshape(N_KV_HEADS, GROUP_SIZE, 2, tq, tq).transpose(0, 2, 3, 1, 4)
    band_p = band_p.reshape(N_KV_HEADS, 2, tq, GROUP_SIZE * tq)
    r8 = jnp.minimum(jnp.arange(8, dtype=jnp.int32), t_new - 1)[:, None]
    c2 = jnp.arange(2 * page, dtype=jnp.int32)[None, :]
    band_s = _band_bias(rel_bias, jnp.maximum(r8 + page - c2, 0)).reshape(N_HEADS * 8, 2 * page)

    ck_t = jnp.transpose(cache_k, (0, 1, 3, 4, 2)).reshape(depth, -1, N_KV_HEADS * HEAD_DIM, page)
    cv_t = jnp.transpose(cache_v, (0, 1, 3, 4, 2)).reshape(depth, -1, N_KV_HEADS * HEAD_DIM, page)
    cki_t = jnp.transpose(cache_idx_k, (0, 1, 3, 2))
    state_t = jnp.swapaxes(state_conv, 1, 2)

    xp = x_prompt.reshape(nb * seq, d)
    xs = jnp.swapaxes(x_sample, 0, 1).reshape(n_s, d)

    def heads_rows(a, width):
        a = a.reshape(t_new, db, -1, width).transpose(1, 2, 0, 3)
        a = jnp.pad(a, ((0, 0), (0, 0), (0, 8 - t_new), (0, 0)), mode="edge")
        return a.reshape(db, -1, width)

    outs = {k: [] for k in ("kp", "vp", "kip", "cp", "ks", "vs", "kis", "cs", "gv")}
    for l in range(depth):
        xp = _ffn(xp, l, *w["f1"], tm=tm_p)
        mi = _mix_in(xp, l, wp, tm=tm_p, n_seq=nb, seq_tiles=seq // tm_p, sample=False)
        att = _attn_prompt(mi, band_p, n_seq=nb, seq=seq, tq=tq, k_sel=k_sel_p)
        xp = _merge_ffn(xp, mi["ca"], att, mi["gm"], mi["gates"], l, w, tm=tm_p)
        kv = mi["kv"].reshape(nb, seq, 2, N_KV_HEADS, HEAD_DIM)
        outs["kp"].append(kv[:, :, 0])
        outs["vp"].append(kv[:, :, 1])
        outs["kip"].append(mi["kw"][:, :IDX_DIM].reshape(nb, seq, IDX_DIM))
        outs["cp"].append(mi["conv_state"])

        xs = _ffn(xs, l, *w["f1"], tm=n_s)
        ms = _mix_in(xs, l, wsm, tm=n_s, n_seq=db, seq_tiles=1, sample=True, state=state_t[l])
        kv_s = jnp.swapaxes(ms["kv"].reshape(t_new, db, 2, N_KV_HEADS * HEAD_DIM), 0, 1)
        ki_s = jnp.swapaxes(ms["kw"][:, :IDX_DIM].reshape(t_new, db, IDX_DIM), 0, 1)
        qi_rows = heads_rows(ms["qi"], IDX_DIM)
        wq_rows = heads_rows(jnp.broadcast_to(
            ms["kw"][:, IDX_DIM:IDX_DIM + N_IDX_HEADS, None], (n_s, N_IDX_HEADS, LANES)
        ).reshape(n_s, N_IDX_HEADS * LANES), LANES)
        qh = heads_rows(ms["q"], HEAD_DIM)
        zero = jnp.zeros_like(qh)
        first = (jnp.arange(N_HEADS * 8) < GROUP_SIZE * 8)[None, :, None]
        q_rows = jnp.concatenate([jnp.where(first, qh, zero), jnp.where(first, zero, qh)], axis=-1)
        new_t = lambda a: jnp.pad(jnp.swapaxes(a, 1, 2), ((0, 0), (0, 0), (0, page - t_new))).astype(BF)
        o = _attn_sample(page_table, qi_rows, wq_rows, q_rows, new_t(ki_s), new_t(kv_s[:, :, 0]),
                         new_t(kv_s[:, :, 1]), band_s, cki_t, ck_t, cv_t, layer=l, k_sel=k_sel_s, n_new=t_new)
        o = o.reshape(db, N_HEADS, 8, N_KV_HEADS, HEAD_DIM)[:, :, :t_new]
        o = jnp.concatenate([o[:, :GROUP_SIZE, :, 0], o[:, GROUP_SIZE:, :, 1]], axis=1)
        att_s = o.transpose(2, 0, 1, 3).reshape(n_s, N_HEADS * HEAD_DIM).astype(BF)
        xs = _merge_ffn(xs, ms["ca"], att_s, ms["gm"], ms["gates"], l, w, tm=n_s)
        outs["ks"].append(kv_s[:, :, 0].reshape(db, t_new, N_KV_HEADS, HEAD_DIM))
        outs["vs"].append(kv_s[:, :, 1].reshape(db, t_new, N_KV_HEADS, HEAD_DIM))
        outs["kis"].append(ki_s)
        outs["cs"].append(jnp.swapaxes(ms["conv_state"], 0, 1))
        outs["gv"].append(jnp.swapaxes(ms["gmlp_v"].reshape(t_new, db, cg), 0, 1))

    yp = xp.reshape(nb, seq, d)
    ys = jnp.swapaxes(xs.reshape(t_new, db, d), 0, 1)
    st = lambda k: jnp.stack(outs[k])
    return (yp, ys, st("kp"), st("vp"), st("kip"), st("cp"),
            st("ks"), st("vs"), st("kis"), st("cs"), st("gv"))
```

```python
import functools
import math

import jax
import jax.numpy as jnp
from jax import lax
from jax.experimental import pallas as pl
from jax.experimental.pallas import tpu as pltpu

F32 = jnp.float32
BF = jnp.bfloat16

N_HEADS = 8
N_KV_HEADS = 2
HEAD_DIM = 64
GROUP_SIZE = N_HEADS // N_KV_HEADS
N_IDX_HEADS = 8
IDX_DIM = 64
TOPK_MAX = 256
N_BUCKETS = 32
MAX_DISTANCE = 128
N_GMLP_GROUPS = 4
GMLP_CHUNK = 128
EPS = 1e-6

LANES = 128
MASKED = -1e30
M_INIT = -1e29
LOG2E = 1.4426950408889634
F32_LOWEST = -3.4028234663852886e38
VMEM_LIMIT = 56 * 1024 * 1024

NT_DIMS = (((1,), (1,)), ((), ()))


def _cparams(n_axes):
    return pltpu.CompilerParams(dimension_semantics=("arbitrary",) * n_axes,
                                vmem_limit_bytes=VMEM_LIMIT)


def _resident(shape, layer=None):
    nd = len(shape)
    if layer is None:
        return pl.BlockSpec(shape, lambda *_: (0,) * nd, pipeline_mode=pl.Buffered(1))
    return pl.BlockSpec((None,) + tuple(shape), lambda *_: (layer,) + (0,) * nd,
                        pipeline_mode=pl.Buffered(1))


def _rms(x, g):
    return x * lax.rsqrt(jnp.mean(x * x, axis=-1, keepdims=True) + EPS) * g


def _layer_norm(x, g, b):
    mu = jnp.mean(x, axis=-1, keepdims=True)
    xc = x - mu
    return xc * lax.rsqrt(jnp.mean(xc * xc, axis=-1, keepdims=True) + EPS) * g + b


def _dot(a, b):
    return jnp.dot(a, b, preferred_element_type=F32)


def _dot_nt(a, bt):
    return lax.dot_general(a, bt, NT_DIMS, preferred_element_type=F32)


def _sigmoid(x):
    return 0.5 * jnp.tanh(0.5 * x) + 0.5


def _silu(x):
    return x * _sigmoid(x)


FF_CHUNK = 256


def _ffn_half_step(x, pre_ref, post_ref, wg_ref, wu_ref, wd_ref):
    h = _rms(x, pre_ref[...]).astype(BF)
    acc = jnp.zeros(x.shape, F32)
    for c in range(wg_ref.shape[1] // FF_CHUNK):
        sl = slice(c * FF_CHUNK, (c + 1) * FF_CHUNK)
        a = _silu(_dot(h, wg_ref[:, sl])) * _dot(h, wu_ref[:, sl])
        acc = acc + _dot(a.astype(BF), wd_ref[sl, :])
    return x + 0.5 * _rms(acc, post_ref[...])


def _ffn_body(x_ref, pre_ref, post_ref, wg_ref, wu_ref, wd_ref, o_ref):
    o_ref[...] = _ffn_half_step(x_ref[...], pre_ref, post_ref, wg_ref, wu_ref, wd_ref)


def _ffn(x, layer, pre, post, wg, wu, wd, *, tm):
    n, d = x.shape
    dff = wg.shape[2]
    return pl.pallas_call(
        _ffn_body,
        out_shape=jax.ShapeDtypeStruct((n, d), F32),
        grid=(n // tm,),
        in_specs=[pl.BlockSpec((tm, d), lambda i: (i, 0)),
                  _resident((1, d), layer), _resident((1, d), layer),
                  _resident((d, dff), layer), _resident((d, dff), layer), _resident((dff, d), layer)],
        out_specs=pl.BlockSpec((tm, d), lambda i: (i, 0)),
        compiler_params=_cparams(1),
        name="ffn",
    )(x, pre, post, wg, wu, wd)


_MIX_IN = ["x", "pre", "wa", "wrow", "wuv", "wg", "cw", "cb", "clg", "clb", "glg", "glb", "ws", "bsb"]
_Q_SCALE = HEAD_DIM ** -0.5 * LOG2E
_HQ = N_HEADS * HEAD_DIM
_HI = N_IDX_HEADS * IDX_DIM
_HKV = N_KV_HEADS * HEAD_DIM


def _mix_in_body(*refs, names, tm, conv_w, sample, n_seq):
    r = dict(zip(names, refs))
    ca_ref, gm_ref, cs_ref, xin_ref = r["ca"], r["gm"], r["conv_state"], r["xin"]
    cw_ref, cb_ref, clg_ref, clb_ref = r["cw"], r["cb"], r["clg"], r["clb"]
    glg_ref, glb_ref, ws_ref, bsb_ref = r["glg"], r["glb"], r["ws"], r["bsb"]

    h = _rms(r["x"][...], r["pre"][...]).astype(BF)

    def projections():
        row = _dot_nt(h, r["wrow"][...])
        if sample:
            r["q"][...] = (row[:, 0:_HQ] * _Q_SCALE).astype(BF)
            r["qi"][...] = row[:, _HQ:_HQ + _HI].astype(BF)
            o = _HQ + _HI
        else:
            col = _dot_nt(r["wcol"][...], h)
            r["qt"][...] = (col[0:_HQ] * _Q_SCALE).astype(BF)
            r["qit"][...] = col[_HQ:_HQ + _HI].astype(BF)
            vt = col[_HQ + _HI:_HQ + _HI + _HKV].astype(BF)
            tq = r["vt"].shape[2]
            for c in range(tm // tq):
                r["vt"][c] = vt[:, c * tq:(c + 1) * tq]
            r["wit"][...] = col[_HQ + _HI + _HKV:_HQ + _HI + _HKV + N_IDX_HEADS]
            r["ki"][...] = row[:, 2 * _HKV:2 * _HKV + IDX_DIM].astype(BF)
            for g in range(N_KV_HEADS):
                r["kg"][g] = row[:, g * HEAD_DIM:(g + 1) * HEAD_DIM].astype(BF)
            o = 0
        r["kv"][...] = row[:, o:o + 2 * _HKV]
        r["kw"][...] = row[:, o + 2 * _HKV:o + 2 * _HKV + LANES]
        r["gates"][...] = _sigmoid(_dot_nt(h, r["wg"][...])).astype(BF)

    wa_ref, wuv_ref = r["wa"], r["wuv"]

    a = _dot_nt(h, wa_ref[...])
    uv = jax.nn.gelu(_dot_nt(h, wuv_ref[...]))
    cc = a.shape[1] // 2
    glu = a[:, :cc] * _sigmoid(a[:, cc:])
    pre_rows = conv_w - 1
    if sample:
        n_t = tm // n_seq
        xin_ref[0:pre_rows] = r["state"][...]
        for t in range(n_t):
            xin_ref[pre_rows + t] = glu[t * n_seq:(t + 1) * n_seq, :]
        ys = []
        for t in range(n_t):
            y = jnp.zeros((n_seq, cc), F32)
            for j in range(conv_w):
                y = y + xin_ref[t + j] * cw_ref[j:j + 1, :]
            ys.append(y)
        y = jnp.concatenate(ys, axis=0)
        cs_ref[...] = xin_ref[n_t:n_t + pre_rows]
        ca_ref[...] = _silu(_layer_norm(y + cb_ref[...], clg_ref[...], clb_ref[...])).astype(BF)
    else:
        halo = 32
        @pl.when(pl.program_id(1) == 0)
        def _():
            xin_ref[0:halo, :] = jnp.zeros((halo, cc), F32)
        xin_ref[halo:halo + tm, :] = glu
        rb = 64
        off = halo - pre_rows
        sh_ref = r["shifted"]
        for b in range(8):
            n_rows = tm + 8 * (len(range(b, conv_w, 8)) - 1)
            sh_ref[b, 0:n_rows, :] = xin_ref[off + b:off + b + n_rows, :]
        cs_ref[...] = xin_ref[halo + tm - pre_rows:halo + tm, :]
        xin_ref[0:halo, :] = xin_ref[tm:tm + halo, :]

    cg = uv.shape[1] // 2
    u = uv[:, :cg]
    vn = _layer_norm(uv[:, cg:], glg_ref[...], glb_ref[...])
    if sample:
        r["gmlp_v"][...] = vn
    vb = vn.astype(BF)

    projections()

    if not sample:
        for r0 in range(0, tm, rb):
            y = jnp.zeros((rb, cc), F32)
            for j in range(conv_w):
                y = y + sh_ref[j % 8, r0 + j - j % 8:r0 + j - j % 8 + rb, :] * cw_ref[j:j + 1, :]
            y = _silu(_layer_norm(y + cb_ref[...], clg_ref[...], clb_ref[...]))
            ca_ref[r0:r0 + rb, :] = y.astype(BF)

    gd = cg // N_GMLP_GROUPS
    chunk = ws_ref.shape[-1]
    ri = lax.broadcasted_iota(jnp.int32, (chunk, chunk), 0)
    ci = lax.broadcasted_iota(jnp.int32, (chunk, chunk), 1)
    wsm = [jnp.where(ci <= ri, ws_ref[g], 0.0).astype(BF) for g in range(N_GMLP_GROUPS)]
    for n in range(tm // chunk):
        rs = slice(n * chunk, (n + 1) * chunk)
        mixed = jnp.concatenate(
            [_dot(wsm[g], vb[rs, g * gd:(g + 1) * gd]) for g in range(N_GMLP_GROUPS)], axis=1)
        gm_ref[rs, :] = (u[rs, :] * (mixed + bsb_ref[...])).astype(BF)


def _mix_in(x, layer, w, *, tm, n_seq, seq_tiles, sample, state=None):
    n, d = x.shape
    conv_w = w["conv_w"].shape[1]
    cc = w["conv_w"].shape[2]
    cg = w["gmlp_ln_g"].shape[2]
    pre_rows = conv_w - 1
    if sample:
        grid = (1,)
        row = lambda i: (0, 0)
        res = lambda shape: _resident(shape, layer)
        cs_shape = (pre_rows, n_seq, cc)
        cs_spec = pl.BlockSpec(cs_shape, lambda i: (0, 0, 0))
        scratch = {"xin": pltpu.VMEM((pre_rows + tm // n_seq, n_seq, cc), F32)}
    else:
        grid = (n_seq, seq_tiles)
        row = lambda b, t: (b * seq_tiles + t, 0)
        res = lambda shape: _resident(shape, layer)
        cs_shape = (n_seq, pre_rows, cc)
        cs_spec = pl.BlockSpec((None, pre_rows, cc), lambda b, t: (b, 0, 0))
        scratch = {"xin": pltpu.VMEM((32 + tm, cc), F32),
                   "shifted": pltpu.VMEM((8, tm + 8 * ((conv_w - 1) // 8), cc), F32)}

    wrow = w["wrow_s"] if sample else w["wrow_p"]
    in_names = list(_MIX_IN)
    args = [x, w["mix_pre"], w["wa"], wrow, w["wuv"], w["wg"], w["conv_w"], w["conv_b"], w["conv_ln_g"],
            w["conv_ln_b"], w["gmlp_ln_g"], w["gmlp_ln_b"], w["ws"], w["bsb"]]
    in_specs = [pl.BlockSpec((tm, d), row)] + [res(a.shape[1:]) for a in args[1:]]
    if sample:
        in_names.append("state")
        in_specs.append(pl.BlockSpec(cs_shape, lambda i: (0, 0, 0)))
        args.append(state)
    else:
        in_names.append("wcol")
        in_specs.append(res(w["wcol"].shape[1:]))
        args.append(w["wcol"])

    rows = lambda wd, dt: ((n, wd), (tm, wd), row, dt)
    outs = {"kv": rows(2 * _HKV, F32), "kw": rows(LANES, F32), "ca": rows(cc, BF), "gm": rows(cg, BF),
            "gates": rows(w["wg"].shape[1], BF), "conv_state": (cs_shape, cs_spec.block_shape, cs_spec.index_map, F32)}
    if sample:
        outs.update(q=rows(_HQ, BF), qi=rows(_HI, BF), gmlp_v=rows(cg, F32))
    else:
        tq = 256
        cols = lambda ht, dt: ((ht, n), (ht, tm), lambda b, t: (0, b * seq_tiles + t), dt)
        outs.update(qt=cols(_HQ, BF), qit=cols(_HI, BF), wit=cols(N_IDX_HEADS, F32), ki=rows(IDX_DIM, BF),
                    kg=((N_KV_HEADS, n, HEAD_DIM), (N_KV_HEADS, tm, HEAD_DIM),
                        lambda b, t: (0, b * seq_tiles + t, 0), BF),
                    vt=((n // tq, _HKV, tq), (tm // tq, _HKV, tq), lambda b, t: (b * seq_tiles + t, 0, 0), BF))
    out_names = list(outs)

    res_out = pl.pallas_call(
        functools.partial(_mix_in_body, names=in_names + out_names + list(scratch), tm=tm, conv_w=conv_w,
                          sample=sample, n_seq=n_seq),
        out_shape=[jax.ShapeDtypeStruct(outs[k][0], outs[k][3]) for k in out_names],
        grid=grid, in_specs=in_specs,
        out_specs=[pl.BlockSpec(outs[k][1], outs[k][2]) for k in out_names],
        scratch_shapes=list(scratch.values()), compiler_params=_cparams(len(grid)),
        name="mix_in_sample" if sample else "mix_in",
    )(*args)
    return dict(zip(out_names, res_out))


def _ordered_bits_to_float(u):
    t = u ^ jnp.int32(-2 ** 31)
    fb = t ^ (lax.shift_right_arithmetic(t, 31) & jnp.int32(0x7FFFFFFF))
    return lax.bitcast_convert_type(fb, F32)


def _row_total(cnt, ones_bf):
    return _dot(cnt.astype(BF), ones_bf)


def _select_topk(sc_ref, nk, rows, width, k_sel):
    nh = width // LANES
    ones_bf = jnp.ones((LANES, LANES), BF)
    kf = float(k_sel)

    def halves(x):
        return [x[:, i * LANES:(i + 1) * LANES] for i in range(nh)]

    def count(pred):
        def body(c, cnt):
            for xh in halves(sc_ref[c]):
                cnt = cnt + jnp.where(pred(xh), 1.0, 0.0)
            return cnt
        return _row_total(lax.fori_loop(0, nk, body, jnp.zeros((rows, LANES), F32)), ones_bf)

    def bit_body(i, u):
        cand = u | jnp.left_shift(jnp.int32(1), 31 - i)
        thr = _ordered_bits_to_float(cand)
        return jnp.where(count(lambda xh: xh >= thr) >= kf, cand, u)

    u = lax.fori_loop(0, 32, bit_body, jnp.zeros((rows, LANES), jnp.int32))
    thr = _ordered_bits_to_float(u)
    thr = jnp.where(thr >= F32_LOWEST, thr, F32_LOWEST)

    n_ge = count(lambda xh: xh >= thr)

    @pl.when(jnp.max(n_ge) > kf)
    def _():
        need = kf - count(lambda xh: xh > thr)
        ri = lax.broadcasted_iota(jnp.int32, (LANES, LANES), 0)
        ci = lax.broadcasted_iota(jnp.int32, (LANES, LANES), 1)
        before = jnp.where(ri < ci, 1.0, 0.0).astype(BF)

        def fix(c, seen):
            out = []
            for xh in halves(sc_ref[c]):
                tie = xh == thr
                tf = jnp.where(tie, 1.0, 0.0).astype(BF)
                rank = seen + _dot(tf, before)
                out.append(jnp.where(jnp.where(tie, rank, -1.0) >= need, -jnp.inf, xh))
                seen = seen + _dot(tf, ones_bf)
            sc_ref[c] = out[0] if nh == 1 else jnp.concatenate(out, axis=1)
            return seen

        lax.fori_loop(0, nk, fix, jnp.zeros((rows, LANES), F32))

    def to_mask(c, carry):
        x = sc_ref[c]
        thr_w = thr if nh == 1 else jnp.concatenate([thr] * nh, axis=1)
        sc_ref[c] = jnp.where(x >= thr_w, 0.0, MASKED)
        return carry

    lax.fori_loop(0, nk, to_mask, 0)


def _select_topk_cols(sc_ref, hi_ref, lo_ref, nk, n_keys, n_q, k_sel):
    kf = float(k_sel)
    groups = n_keys // 8
    slabs = n_keys // 16
    i16 = jnp.int16
    low_mask = jnp.int32(0xFFFF)
    bias16 = jnp.int32(1 << 15)

    def split(c, carry):
        bits = lax.bitcast_convert_type(sc_ref[c], jnp.int32)
        key = bits ^ (lax.shift_right_arithmetic(bits, 31) & jnp.int32(0x7FFFFFFF))
        hi_ref[c] = lax.shift_right_arithmetic(key, 16).astype(i16)
        lo_ref[c] = ((key & low_mask) - bias16).astype(i16)
        return carry

    lax.fori_loop(0, nk, split, 0)

    def count16(ref, pred):
        def body(c, acc):
            for s in range(slabs):
                acc = acc + jnp.where(pred(ref[c, s * 16:(s + 1) * 16, :]), i16(1), i16(0))
            return acc
        acc = lax.fori_loop(0, nk, body, jnp.zeros((16, n_q), i16))
        return jnp.sum(acc.astype(F32), axis=0, keepdims=True)

    def tile16(v):
        return jnp.broadcast_to(v, (16, n_q)).astype(i16)

    def search16(ref, need):
        def bit_body(i, u):
            cand = u | jnp.left_shift(jnp.int32(1), 15 - i)
            c16 = tile16(cand - bias16)
            return jnp.where(count16(ref, lambda x: x >= c16) >= need, cand, u)
        return lax.fori_loop(0, 16, bit_body, jnp.zeros((1, n_q), jnp.int32)) - bias16

    hi_k = search16(hi_ref, kf)
    hi16 = tile16(hi_k)
    need_lo = kf - count16(hi_ref, lambda x: x > hi16)

    def keep_boundary(c, carry):
        for s in range(slabs):
            rows = slice(s * 16, (s + 1) * 16)
            lo_ref[c, rows, :] = jnp.where(hi_ref[c, rows, :] == hi16, lo_ref[c, rows, :], i16(-(1 << 15)))
        return carry

    lax.fori_loop(0, nk, keep_boundary, 0)
    lo_k = search16(lo_ref, need_lo)
    key_k = jnp.left_shift(hi_k, 16) | (lo_k + bias16)
    thr = lax.bitcast_convert_type(
        key_k ^ (lax.shift_right_arithmetic(key_k, 31) & jnp.int32(0x7FFFFFFF)), F32)
    thr = jnp.where(thr >= F32_LOWEST, thr, F32_LOWEST)

    def count(pred):
        def body(c, cnt):
            x = sc_ref[c].reshape(groups, 8, n_q)
            return cnt + jnp.sum(jnp.where(pred(x), 1.0, 0.0), axis=0)
        cnt = lax.fori_loop(0, nk, body, jnp.zeros((8, n_q), F32))
        return jnp.sum(cnt, axis=0, keepdims=True)

    thr8 = jnp.broadcast_to(thr, (8, n_q))[None]

    @pl.when(jnp.max(count(lambda x: x >= thr8)) > kf)
    def _():
        need = kf - count(lambda x: x > thr8)
        ri = lax.broadcasted_iota(jnp.int32, (n_keys, n_keys), 0)
        ci = lax.broadcasted_iota(jnp.int32, (n_keys, n_keys), 1)
        before = jnp.where(ci < ri, 1.0, 0.0).astype(BF)

        def fix(c, seen):
            x = sc_ref[c]
            tie = x == thr
            tf = jnp.where(tie, 1.0, 0.0)
            rank = seen + _dot(before, tf.astype(BF))
            sc_ref[c] = jnp.where(jnp.where(tie, rank, -1.0) >= need, -jnp.inf, x)
            return seen + jnp.sum(tf, axis=0, keepdims=True)

        lax.fori_loop(0, nk, fix, jnp.zeros((1, n_q), F32))

    def to_mask(c, carry):
        sc_ref[c] = jnp.where(sc_ref[c] >= thr, 0.0, MASKED)
        return carry

    lax.fori_loop(0, nk, to_mask, 0)


def _attn_prompt_body(qt_ref, qit_ref, wit_ref, ki_ref, kg_ref, vt_ref, band_ref, o_ref, sc_ref, hi_ref, lo_ref,
                      *, tq, k_sel, heads_per_stream):
    j = pl.program_id(1)
    nk = j + 1

    def heads_along_lanes(x, heads, width):
        return jnp.concatenate([x[h * width:(h + 1) * width, :] for h in heads], axis=1)

    qi_all = heads_along_lanes(qit_ref[...], range(N_IDX_HEADS), IDX_DIM)
    w_all = heads_along_lanes(wit_ref[...], range(N_IDX_HEADS), 1)
    halves = tq // LANES
    q_pos = lax.broadcasted_iota(jnp.int32, (tq, LANES), 1) + j * tq

    def score_chunk(c, carry):
        k0 = pl.multiple_of(c * tq, tq)
        kic = ki_ref[pl.ds(k0, tq), :]
        k_pos = lax.broadcasted_iota(jnp.int32, (tq, LANES), 0) + k0
        for i in range(halves):
            acc = None
            for h in range(N_IDX_HEADS):
                ls = slice(h * tq + i * LANES, h * tq + (i + 1) * LANES)
                term = jnp.maximum(_dot(kic, qi_all[:, ls]), 0.0) * w_all[:, ls]
                acc = term if acc is None else acc + term
            sc_ref[c, :, i * LANES:(i + 1) * LANES] = jnp.where(k_pos <= q_pos + i * LANES, acc, -jnp.inf)
        return carry

    lax.fori_loop(0, nk, score_chunk, 0)
    _select_topk_cols(sc_ref, hi_ref, lo_ref, nk, tq, tq, k_sel)

    qt = qt_ref[...]
    n_streams = N_HEADS // heads_per_stream
    per_group = GROUP_SIZE // heads_per_stream
    wide = heads_per_stream * tq
    q_s = [heads_along_lanes(qt, range(i * heads_per_stream, (i + 1) * heads_per_stream), HEAD_DIM)
           for i in range(n_streams)]

    def chunk_step(c, carry, bias):
        k0 = pl.multiple_of(c * tq, tq)
        mask = jnp.concatenate([sc_ref[c]] * heads_per_stream, axis=1)
        logits = []
        for i in range(n_streams):
            g = i // per_group
            s = _dot(kg_ref[g, pl.ds(k0, tq), :], q_s[i]) + mask
            if bias is not None:
                s = s + bias(g, slice((i % per_group) * wide, (i % per_group + 1) * wide))
            logits.append(s)
        stats = []
        for i in range(n_streams):
            m, l, _ = carry[i]
            m_new = jnp.maximum(m, jnp.max(logits[i], axis=0, keepdims=True))
            alpha = jnp.exp2(m - m_new)
            p = jnp.exp2(logits[i] - m_new)
            stats.append((m_new, alpha, alpha * l + jnp.sum(p, axis=0, keepdims=True), p.astype(BF)))
        out = []
        for i in range(n_streams):
            g = i // per_group
            m_new, alpha, l_new, p = stats[i]
            vt = vt_ref[c, g * HEAD_DIM:(g + 1) * HEAD_DIM, :]
            out.append((m_new, l_new, alpha * carry[i][2] + _dot(vt, p)))
        return out

    carry = [(jnp.full((1, wide), M_INIT, F32), jnp.zeros((1, wide), F32), jnp.zeros((HEAD_DIM, wide), F32))
             for _ in range(n_streams)]
    j_prev = jnp.maximum(j - 1, 0)
    prev_mask = jnp.where(j > 0, 0.0, MASKED)
    carry = lax.fori_loop(0, j_prev, lambda c, carry: chunk_step(c, carry, None), carry)
    carry = chunk_step(j_prev, carry, lambda g, ls: band_ref[g, 0, :, ls] + prev_mask)
    carry = chunk_step(j, carry, lambda g, ls: band_ref[g, 1, :, ls])
    o_t = []
    for m, l, acc in carry:
        o_s = acc / l
        o_t += [o_s[:, hh * tq:(hh + 1) * tq] for hh in range(heads_per_stream)]
    o_ref[...] = jnp.concatenate(o_t, axis=0).T.astype(BF)


def _attn_prompt(mi, band, *, n_seq, seq, tq, k_sel):
    nt = seq // tq
    col = lambda b, t: (0, b * nt + t)
    hd = N_HEADS * HEAD_DIM
    return pl.pallas_call(
        functools.partial(_attn_prompt_body, tq=tq, k_sel=k_sel, heads_per_stream=2),
        out_shape=jax.ShapeDtypeStruct((n_seq * seq, hd), BF),
        grid=(n_seq, nt),
        in_specs=[pl.BlockSpec((hd, tq), col), pl.BlockSpec((N_IDX_HEADS * IDX_DIM, tq), col),
                  pl.BlockSpec((N_IDX_HEADS, tq), col),
                  pl.BlockSpec((seq, IDX_DIM), lambda b, t: (b, 0)),
                  pl.BlockSpec((N_KV_HEADS, seq, HEAD_DIM), lambda b, t: (0, b, 0)),
                  pl.BlockSpec((nt, N_KV_HEADS * HEAD_DIM, tq), lambda b, t: (b, 0, 0)),
                  _resident(band.shape)],
        out_specs=pl.BlockSpec((tq, hd), lambda b, t: (b * nt + t, 0)),
        scratch_shapes=[pltpu.VMEM((nt, tq, tq), F32), pltpu.VMEM((nt, tq, tq), jnp.int16),
                        pltpu.VMEM((nt, tq, tq), jnp.int16)],
        compiler_params=_cparams(2),
        name="attn_prompt",
    )(mi["qt"], mi["qit"], mi["wit"], mi["ki"], mi["kg"], mi["vt"], band)


def _page_stream(cache_ref, buf_ref, sem_ref, pt_ref, layer, n_pages, page):
    def copy(seq, slot, p):
        return pltpu.make_async_copy(cache_ref.at[layer, pt_ref[seq, p]],
                                     buf_ref.at[slot, :, p * page:(p + 1) * page], sem_ref.at[slot])

    def start(seq, slot):
        for p in range(n_pages):
            copy(seq, slot, p).start()

    def wait(seq, slot):
        for p in range(n_pages):
            copy(seq, slot, p).wait()

    return start, wait


def _double_buffered(streams):
    i = pl.program_id(0)
    slot = i % 2

    @pl.when(i == 0)
    def _():
        for start, _ in streams:
            start(0, 0)

    @pl.when(i + 1 < pl.num_programs(0))
    def _():
        for start, _ in streams:
            start(i + 1, 1 - slot)

    for _, wait in streams:
        wait(i, slot)
    return slot


def _idx_scores_body(pt_ref, qi_ref, wq_ref, kin_ref, cki_ref, o_ref, buf_ref, sem_ref,
                     *, layer, n_pages, page, chunk):
    slot = _double_buffered([_page_stream(cki_ref, buf_ref, sem_ref, pt_ref, layer, n_pages, page)])
    qi = qi_ref[...]
    wq = wq_ref[...]
    rows_t = qi.shape[0] // N_IDX_HEADS
    n_new = o_ref.shape[0]
    past = n_pages * page

    def head_sum(s):
        w = s.shape[1]
        wgt = wq if w == LANES else jnp.concatenate([wq] * (w // LANES), axis=1)
        s = (jnp.maximum(s, 0.0) * wgt).reshape(N_IDX_HEADS, rows_t, w)
        acc = s[0]
        for h in range(1, N_IDX_HEADS):
            acc = acc + s[h]
        return acc[0:n_new]

    for c in range(past // chunk):
        cs = slice(c * chunk, (c + 1) * chunk)
        o_ref[:, cs] = head_sum(_dot(qi, buf_ref[slot, :, cs].astype(BF)))
    t = lax.broadcasted_iota(jnp.int32, (n_new, page), 0)
    cc = lax.broadcasted_iota(jnp.int32, (n_new, page), 1)
    o_ref[:, past:past + page] = jnp.where(cc <= t, head_sum(_dot(qi, kin_ref[...])), -jnp.inf)


def _select_body(sc_ref, o_ref, *, k_sel):
    o_ref[...] = sc_ref[...]
    _select_topk(o_ref, 1, o_ref.shape[1], o_ref.shape[2], k_sel)


def _attend_sample_body(pt_ref, q_ref, mask_ref, band_ref, kn_ref, vn_ref, ck_ref, cv_ref, o_ref,
                        kbuf_ref, vbuf_ref, s_ref, ksem_ref, vsem_ref, *, layer, n_pages, page, chunk):
    slot = _double_buffered([_page_stream(ck_ref, kbuf_ref, ksem_ref, pt_ref, layer, n_pages, page),
                             _page_stream(cv_ref, vbuf_ref, vsem_ref, pt_ref, layer, n_pages, page)])
    q = q_ref[...]
    rows = q.shape[0]
    rows_t = rows // N_HEADS
    n_new = mask_ref.shape[0]
    past = n_pages * page
    n_chunks = past // chunk

    def add_mask(s, m):
        w = s.shape[1]
        m = jnp.concatenate([m] + [m[n_new - 1:n_new]] * (rows_t - n_new), axis=0)
        return (s.reshape(N_HEADS, rows_t, w) + m[None]).reshape(rows, w)

    def lane_fold(x, fn, init):
        for i in range(x.shape[1] // LANES):
            init = fn(init, x[:, i * LANES:(i + 1) * LANES])
        return init

    mx = jnp.full((rows, LANES), M_INIT, F32)
    for c in range(n_chunks):
        cs = slice(c * chunk, (c + 1) * chunk)
        s = add_mask(_dot(q, kbuf_ref[slot, :, cs].astype(BF)), mask_ref[:, cs])
        if c == n_chunks - 1:
            s = jnp.concatenate([s[:, :chunk - page], s[:, chunk - page:] + band_ref[:, 0:page]], axis=1)
        s_ref[:, cs] = s
        mx = lane_fold(s, jnp.maximum, mx)
    s_new = add_mask(_dot(q, kn_ref[...]), mask_ref[:, past:past + page]) + band_ref[:, page:2 * page]
    m = jnp.max(jnp.maximum(mx, s_new), axis=1, keepdims=True)

    p_new = jnp.exp2(s_new - m)
    l = p_new
    acc = lax.dot_general(p_new.astype(BF), vn_ref[...], NT_DIMS, preferred_element_type=F32)
    for c in range(n_chunks):
        cs = slice(c * chunk, (c + 1) * chunk)
        p = jnp.exp2(s_ref[:, cs] - m)
        l = lane_fold(p, jnp.add, l)
        acc = acc + lax.dot_general(p.astype(BF), vbuf_ref[slot, :, cs].astype(BF), NT_DIMS,
                                    preferred_element_type=F32)
    o_ref[...] = acc / jnp.sum(l, axis=1, keepdims=True)


def _attn_sample(page_table, qi_rows, wq_rows, q_rows, ki_new_t, k_new_t, v_new_t, band, cki_t, ck_t, cv_t,
                 *, layer, k_sel, n_new):
    n_seq, n_pages = page_table.shape
    page = ck_t.shape[3]
    rows = q_rows.shape[1]
    past = n_pages * page
    width = past + page
    chunk = 1024 if past % 1024 == 0 else page
    per_seq3 = lambda b, pt: (b, 0, 0)
    any_spec = pl.BlockSpec(memory_space=pl.ANY)

    scores = pl.pallas_call(
        functools.partial(_idx_scores_body, layer=layer, n_pages=n_pages, page=page, chunk=chunk),
        out_shape=jax.ShapeDtypeStruct((n_seq, n_new, width), F32),
        grid_spec=pltpu.PrefetchScalarGridSpec(
            num_scalar_prefetch=1, grid=(n_seq,),
            in_specs=[pl.BlockSpec((None, rows, IDX_DIM), per_seq3),
                      pl.BlockSpec((None, rows, LANES), per_seq3),
                      pl.BlockSpec((None, IDX_DIM, page), per_seq3),
                      any_spec],
            out_specs=pl.BlockSpec((None, n_new, width), per_seq3),
            scratch_shapes=[pltpu.VMEM((2, IDX_DIM, past), F32), pltpu.SemaphoreType.DMA((2,))]),
        compiler_params=_cparams(1),
        name="idx_scores_sample",
    )(page_table, qi_rows, wq_rows, ki_new_t, cki_t)

    n_sel = n_seq * n_new
    sel_rows = 64 if n_sel % 64 == 0 else n_sel
    mask = pl.pallas_call(
        functools.partial(_select_body, k_sel=k_sel),
        out_shape=jax.ShapeDtypeStruct((1, n_sel, width), F32),
        grid=(n_sel // sel_rows,),
        in_specs=[pl.BlockSpec((1, sel_rows, width), lambda i: (0, i, 0))],
        out_specs=pl.BlockSpec((1, sel_rows, width), lambda i: (0, i, 0)),
        compiler_params=_cparams(1),
        name="select_sample",
    )(scores.reshape(1, n_sel, width)).reshape(n_seq, n_new, width)

    return pl.pallas_call(
        functools.partial(_attend_sample_body, layer=layer, n_pages=n_pages, page=page, chunk=chunk),
        out_shape=jax.ShapeDtypeStruct((n_seq, rows, LANES), F32),
        grid_spec=pltpu.PrefetchScalarGridSpec(
            num_scalar_prefetch=1, grid=(n_seq,),
            in_specs=[pl.BlockSpec((None, rows, LANES), per_seq3),
                      pl.BlockSpec((None, n_new, width), per_seq3),
                      pl.BlockSpec(band.shape, lambda b, pt: (0, 0), pipeline_mode=pl.Buffered(1)),
                      pl.BlockSpec((None, LANES, page), per_seq3),
                      pl.BlockSpec((None, LANES, page), per_seq3),
                      any_spec, any_spec],
            out_specs=pl.BlockSpec((None, rows, LANES), per_seq3),
            scratch_shapes=[pltpu.VMEM((2, LANES, past), F32), pltpu.VMEM((2, LANES, past), F32),
                            pltpu.VMEM((rows, past), F32),
                            pltpu.SemaphoreType.DMA((2,)), pltpu.SemaphoreType.DMA((2,))]),
        compiler_params=_cparams(1),
        name="attend_sample",
    )(page_table, q_rows, mask, band, k_new_t, v_new_t, ck_t, cv_t)


def _merge_ffn_body(x_ref, ca_ref, att_ref, gm_ref, gt_ref, wco_ref, wao_ref, wgo_ref, wout_ref, post_ref,
                    fpre_ref, fpost_ref, wg_ref, wu_ref, wd_ref, o_ref):
    d = x_ref.shape[1]
    gt = gt_ref[...].astype(F32)
    m = (gt[:, 0:d] * _dot(ca_ref[...], wco_ref[...])
         + gt[:, d:2 * d] * _dot(att_ref[...], wao_ref[...])
         + gt[:, 2 * d:3 * d] * _dot(gm_ref[...], wgo_ref[...]))
    x = x_ref[...] + _rms(_dot(m.astype(BF), wout_ref[...]), post_ref[...])
    o_ref[...] = _ffn_half_step(x, fpre_ref, fpost_ref, wg_ref, wu_ref, wd_ref)


def _merge_ffn(x, ca, att, gm, gates, layer, w, *, tm):
    n, d = x.shape
    row = lambda i: (i, 0)
    weights = [w["wco"], w["wao"], w["wgo"], w["wout"], w["mix_post"], *w["f2"]]
    return pl.pallas_call(
        _merge_ffn_body,
        out_shape=jax.ShapeDtypeStruct((n, d), F32),
        grid=(n // tm,),
        in_specs=[pl.BlockSpec((tm, d), row), pl.BlockSpec((tm, ca.shape[1]), row),
                  pl.BlockSpec((tm, att.shape[1]), row), pl.BlockSpec((tm, gm.shape[1]), row),
                  pl.BlockSpec((tm, gates.shape[1]), row)] + [_resident(a.shape[1:], layer) for a in weights],
        out_specs=pl.BlockSpec((tm, d), row),
        compiler_params=_cparams(1),
        name="merge_ffn",
    )(x, ca, att, gm, gates, *weights)


def _band_body(rb_ref, bucket_ref, o_ref):
    h = pl.program_id(0)
    bucket = bucket_ref[...]
    far = rb_ref[N_BUCKETS - 1, h]
    tile = jnp.zeros(bucket.shape, F32)
    for bkt in range(N_BUCKETS):
        tile = jnp.where(bucket == bkt, rb_ref[bkt, h] - far, tile)
    o_ref[...] = tile * LOG2E


def _band_bias(rel_bias, dist):
    max_exact = N_BUCKETS // 2
    d = jnp.maximum(dist, 1).astype(F32)
    large = max_exact + (jnp.log(d / max_exact) / math.log(MAX_DISTANCE / max_exact)
                         * (N_BUCKETS - max_exact)).astype(jnp.int32)
    bucket = jnp.where(dist < max_exact, dist, jnp.minimum(large, N_BUCKETS - 1))
    r, c = dist.shape
    return pl.pallas_call(
        _band_body,
        out_shape=jax.ShapeDtypeStruct((N_HEADS, r, c), F32),
        grid=(N_HEADS,),
        in_specs=[pl.BlockSpec(memory_space=pltpu.SMEM), pl.BlockSpec((r, c), lambda h: (0, 0))],
        out_specs=pl.BlockSpec((None, r, c), lambda h: (h, 0, 0)),
        compiler_params=_cparams(1),
        name="band_bias",
    )(rel_bias, bucket)


def _prep_weights(p):
    d = p["w_in"].shape[1]
    cc = p["conv_w"].shape[2]
    cg = p["gmlp_ln_g"].shape[1]
    hq = N_HEADS * HEAD_DIM
    hk = N_KV_HEADS * HEAD_DIM
    hi = N_IDX_HEADS * IDX_DIM
    win_t = jnp.swapaxes(p["w_in"], 1, 2).astype(BF)
    o = 0
    cols = {}
    for name, wd in (("a", 2 * cc), ("q", hq), ("k", hk), ("v", hk), ("qi", hi), ("ki", IDX_DIM),
                     ("wi", N_IDX_HEADS), ("uv", 2 * cg), ("g", 3 * d)):
        cols[name] = win_t[:, o:o + wd, :]
        o += wd
    pad = jnp.zeros((win_t.shape[0], LANES - IDX_DIM - N_IDX_HEADS, d), BF)
    vec = lambda a: a[:, None, :]
    kvw = [cols["k"], cols["v"], cols["ki"], cols["wi"], pad]
    return {
        "wa": cols["a"],
        "wrow_p": jnp.concatenate(kvw, axis=1),
        "wrow_s": jnp.concatenate([cols["q"], cols["qi"]] + kvw, axis=1),
        "wcol": jnp.concatenate([cols["q"], cols["qi"], cols["v"], cols["wi"]], axis=1),
        "wuv": cols["uv"], "wg": cols["g"],
        "mix_pre": vec(p["mix_norm_pre"]), "mix_post": vec(p["mix_norm_post"]),
        "conv_w": p["conv_w"], "conv_b": vec(p["conv_b"]),
        "conv_ln_g": vec(p["conv_ln_g"]), "conv_ln_b": vec(p["conv_ln_b"]),
        "gmlp_ln_g": vec(p["gmlp_ln_g"]), "gmlp_ln_b": vec(p["gmlp_ln_b"]),
        "wco": p["w_conv_out"].astype(BF), "wao": p["w_attn_out"].astype(BF),
        "wgo": p["w_gmlp_out"].astype(BF), "wout": p["w_out"].astype(BF),
        "f1": (vec(p["ffn1_norm_pre"]), vec(p["ffn1_norm_post"]), p["ffn1_w_gate"].astype(BF),
               p["ffn1_w_up"].astype(BF), p["ffn1_w_down"].astype(BF)),
        "f2": (vec(p["ffn2_norm_pre"]), vec(p["ffn2_norm_post"]), p["ffn2_w_gate"].astype(BF),
               p["ffn2_w_up"].astype(BF), p["ffn2_w_down"].astype(BF)),
    }


def _gmlp_spatial(ws, bs, rows, n_seq):
    gd = LANES
    if n_seq is None:
        w = ws
        b = jnp.swapaxes(bs, 1, 2)
    else:
        t = rows // n_seq
        eye = jnp.eye(n_seq, dtype=ws.dtype)
        w = jnp.einsum("lgts,bc->lgtbsc", ws[:, :, :t, :t], eye).reshape(ws.shape[0], ws.shape[1], rows, rows)
        b = jnp.repeat(jnp.swapaxes(bs[:, :, :t], 1, 2), n_seq, axis=1)
    return w, jnp.repeat(b, gd, axis=2)


def kernel(x_prompt, x_sample, cache_k, cache_v, cache_idx_k, state_conv, page_table,
           ffn1_norm_pre, ffn1_norm_post, ffn1_w_gate, ffn1_w_up, ffn1_w_down,
           mix_norm_pre, mix_norm_post, w_in, conv_w, conv_b, conv_ln_g, conv_ln_b,
           w_conv_out, w_attn_out, rel_bias, gmlp_ln_g, gmlp_ln_b, gmlp_ws, gmlp_bs,
           w_gmlp_out, w_out, ffn2_norm_pre, ffn2_norm_post, ffn2_w_gate, ffn2_w_up, ffn2_w_down):
    params = dict(ffn1_norm_pre=ffn1_norm_pre, ffn1_norm_post=ffn1_norm_post, ffn1_w_gate=ffn1_w_gate,
                  ffn1_w_up=ffn1_w_up, ffn1_w_down=ffn1_w_down, mix_norm_pre=mix_norm_pre,
                  mix_norm_post=mix_norm_post, w_in=w_in, conv_w=conv_w, conv_b=conv_b,
                  conv_ln_g=conv_ln_g, conv_ln_b=conv_ln_b, w_conv_out=w_conv_out, w_attn_out=w_attn_out,
                  gmlp_ln_g=gmlp_ln_g, gmlp_ln_b=gmlp_ln_b, w_gmlp_out=w_gmlp_out, w_out=w_out,
                  ffn2_norm_pre=ffn2_norm_pre, ffn2_norm_post=ffn2_norm_post, ffn2_w_gate=ffn2_w_gate,
                  ffn2_w_up=ffn2_w_up, ffn2_w_down=ffn2_w_down)
    depth = w_in.shape[0]
    nb, seq, d = x_prompt.shape
    db, t_new, _ = x_sample.shape
    page = cache_k.shape[2]
    n_pages = page_table.shape[1]
    past = n_pages * page
    n_s = db * t_new
    cc = conv_w.shape[2]
    cg = gmlp_ln_g.shape[1]

    w = _prep_weights(params)
    wp = dict(w)
    wp["ws"], wp["bsb"] = _gmlp_spatial(gmlp_ws, gmlp_bs, GMLP_CHUNK, None)
    wsm = dict(w)
    wsm["ws"], wsm["bsb"] = _gmlp_spatial(gmlp_ws, gmlp_bs, n_s, db)

    tq = 256
    tm_p = 512 if seq % 512 == 0 else 256
    k_sel_p = min(TOPK_MAX, seq // 4)
    k_sel_s = min(TOPK_MAX, (past + t_new) // 4)

    r = jnp.arange(tq, dtype=jnp.int32)[None, :]
    c = jnp.arange(2 * tq, dtype=jnp.int32)[:, None]
    band_p = _band_bias(rel_bias, jnp.maximum(tq + r - c, 0))
    band_p = band_p.reshape(N_KV_HEADS, GROUP_SIZE, 2, tq, tq).transpose(0, 2, 3, 1, 4)
    band_p = band_p.reshape(N_KV_HEADS, 2, tq, GROUP_SIZE * tq)
    r8 = jnp.minimum(jnp.arange(8, dtype=jnp.int32), t_new - 1)[:, None]
    c2 = jnp.arange(2 * page, dtype=jnp.int32)[None, :]
    band_s = _band_bias(rel_bias, jnp.maximum(r8 + page - c2, 0)).reshape(N_HEADS * 8, 2 * page)

    ck_t = jnp.transpose(cache_k, (0, 1, 3, 4, 2)).reshape(depth, -1, N_KV_HEADS * HEAD_DIM, page)
    cv_t = jnp.transpose(cache_v, (0, 1, 3, 4, 2)).reshape(depth, -1, N_KV_HEADS * HEAD_DIM, page)
    cki_t = jnp.transpose(cache_idx_k, (0, 1, 3, 2))
    state_t = jnp.swapaxes(state_conv, 1, 2)

    xp = x_prompt.reshape(nb * seq, d)
    xs = jnp.swapaxes(x_sample, 0, 1).reshape(n_s, d)

    def heads_rows(a, width):
        a = a.reshape(t_new, db, -1, width).transpose(1, 2, 0, 3)
        a = jnp.pad(a, ((0, 0), (0, 0), (0, 8 - t_new), (0, 0)), mode="edge")
        return a.reshape(db, -1, width)

    outs = {k: [] for k in ("kp", "vp", "kip", "cp", "ks", "vs", "kis", "cs", "gv")}
    for l in range(depth):
        xp = _ffn(xp, l, *w["f1"], tm=tm_p)
        mi = _mix_in(xp, l, wp, tm=tm_p, n_seq=nb, seq_tiles=seq // tm_p, sample=False)
        att = _attn_prompt(mi, band_p, n_seq=nb, seq=seq, tq=tq, k_sel=k_sel_p)
        xp = _merge_ffn(xp, mi["ca"], att, mi["gm"], mi["gates"], l, w, tm=tm_p)
        kv = mi["kv"].reshape(nb, seq, 2, N_KV_HEADS, HEAD_DIM)
        outs["kp"].append(kv[:, :, 0])
        outs["vp"].append(kv[:, :, 1])
        outs["kip"].append(mi["kw"][:, :IDX_DIM].reshape(nb, seq, IDX_DIM))
        outs["cp"].append(mi["conv_state"])

        xs = _ffn(xs, l, *w["f1"], tm=n_s)
        ms = _mix_in(xs, l, wsm, tm=n_s, n_seq=db, seq_tiles=1, sample=True, state=state_t[l])
        kv_s = jnp.swapaxes(ms["kv"].reshape(t_new, db, 2, N_KV_HEADS * HEAD_DIM), 0, 1)
        ki_s = jnp.swapaxes(ms["kw"][:, :IDX_DIM].reshape(t_new, db, IDX_DIM), 0, 1)
        qi_rows = heads_rows(ms["qi"], IDX_DIM)
        wq_rows = heads_rows(jnp.broadcast_to(
            ms["kw"][:, IDX_DIM:IDX_DIM + N_IDX_HEADS, None], (n_s, N_IDX_HEADS, LANES)
        ).reshape(n_s, N_IDX_HEADS * LANES), LANES)
        qh = heads_rows(ms["q"], HEAD_DIM)
        zero = jnp.zeros_like(qh)
        first = (jnp.arange(N_HEADS * 8) < GROUP_SIZE * 8)[None, :, None]
        q_rows = jnp.concatenate([jnp.where(first, qh, zero), jnp.where(first, zero, qh)], axis=-1)
        new_t = lambda a: jnp.pad(jnp.swapaxes(a, 1, 2), ((0, 0), (0, 0), (0, page - t_new))).astype(BF)
        o = _attn_sample(page_table, qi_rows, wq_rows, q_rows, new_t(ki_s), new_t(kv_s[:, :, 0]),
                         new_t(kv_s[:, :, 1]), band_s, cki_t, ck_t, cv_t, layer=l, k_sel=k_sel_s, n_new=t_new)
        o = o.reshape(db, N_HEADS, 8, N_KV_HEADS, HEAD_DIM)[:, :, :t_new]
        o = jnp.concatenate([o[:, :GROUP_SIZE, :, 0], o[:, GROUP_SIZE:, :, 1]], axis=1)
        att_s = o.transpose(2, 0, 1, 3).reshape(n_s, N_HEADS * HEAD_DIM).astype(BF)
        xs = _merge_ffn(xs, ms["ca"], att_s, ms["gm"], ms["gates"], l, w, tm=n_s)
        outs["ks"].append(kv_s[:, :, 0].reshape(db, t_new, N_KV_HEADS, HEAD_DIM))
        outs["vs"].append(kv_s[:, :, 1].reshape(db, t_new, N_KV_HEADS, HEAD_DIM))
        outs["kis"].append(ki_s)
        outs["cs"].append(jnp.swapaxes(ms["conv_state"], 0, 1))
        outs["gv"].append(jnp.swapaxes(ms["gmlp_v"].reshape(t_new, db, cg), 0, 1))

    yp = xp.reshape(nb, seq, d)
    ys = jnp.swapaxes(xs.reshape(t_new, db, d), 0, 1)
    st = lambda k: jnp.stack(outs[k])
    return (yp, ys, st("kp"), st("vp"), st("kip"), st("cp"),
            st("ks"), st("vs"), st("kis"), st("cs"), st("gv"))
```

```python
import functools
import math

import jax
import jax.numpy as jnp
from jax import lax
from jax.experimental import pallas as pl
from jax.experimental.pallas import tpu as pltpu

F32 = jnp.float32
BF = jnp.bfloat16

N_HEADS = 8
N_KV_HEADS = 2
HEAD_DIM = 64
GROUP_SIZE = N_HEADS // N_KV_HEADS
N_IDX_HEADS = 8
IDX_DIM = 64
TOPK_MAX = 256
N_BUCKETS = 32
MAX_DISTANCE = 128
N_GMLP_GROUPS = 4
GMLP_CHUNK = 128
EPS = 1e-6

LANES = 128
MASKED = -1e30
M_INIT = -1e29
LOG2E = 1.4426950408889634
F32_LOWEST = -3.4028234663852886e38
VMEM_LIMIT = 56 * 1024 * 1024

NT_DIMS = (((1,), (1,)), ((), ()))


def _cparams(n_axes):
    return pltpu.CompilerParams(dimension_semantics=("arbitrary",) * n_axes,
                                vmem_limit_bytes=VMEM_LIMIT)


def _resident(shape, layer=None):
    nd = len(shape)
    if layer is None:
        return pl.BlockSpec(shape, lambda *_: (0,) * nd, pipeline_mode=pl.Buffered(1))
    return pl.BlockSpec((None,) + tuple(shape), lambda *_: (layer,) + (0,) * nd,
                        pipeline_mode=pl.Buffered(1))


def _rms(x, g):
    return x * lax.rsqrt(jnp.mean(x * x, axis=-1, keepdims=True) + EPS) * g


def _layer_norm(x, g, b):
    mu = jnp.mean(x, axis=-1, keepdims=True)
    xc = x - mu
    return xc * lax.rsqrt(jnp.mean(xc * xc, axis=-1, keepdims=True) + EPS) * g + b


def _dot(a, b):
    return jnp.dot(a, b, preferred_element_type=F32)


def _dot_nt(a, bt):
    return lax.dot_general(a, bt, NT_DIMS, preferred_element_type=F32)


def _sigmoid(x):
    return 0.5 * jnp.tanh(0.5 * x) + 0.5


def _silu(x):
    return x * _sigmoid(x)


FF_CHUNK = 256


def _ffn_half_step(x, pre_ref, post_ref, wg_ref, wu_ref, wd_ref):
    h = _rms(x, pre_ref[...]).astype(BF)
    acc = jnp.zeros(x.shape, F32)
    for c in range(wg_ref.shape[1] // FF_CHUNK):
        sl = slice(c * FF_CHUNK, (c + 1) * FF_CHUNK)
        a = _silu(_dot(h, wg_ref[:, sl])) * _dot(h, wu_ref[:, sl])
        acc = acc + _dot(a.astype(BF), wd_ref[sl, :])
    return x + 0.5 * _rms(acc, post_ref[...])


def _ffn_body(x_ref, pre_ref, post_ref, wg_ref, wu_ref, wd_ref, o_ref):
    o_ref[...] = _ffn_half_step(x_ref[...], pre_ref, post_ref, wg_ref, wu_ref, wd_ref)


def _ffn(x, layer, pre, post, wg, wu, wd, *, tm):
    n, d = x.shape
    dff = wg.shape[2]
    return pl.pallas_call(
        _ffn_body,
        out_shape=jax.ShapeDtypeStruct((n, d), F32),
        grid=(n // tm,),
        in_specs=[pl.BlockSpec((tm, d), lambda i: (i, 0)),
                  _resident((1, d), layer), _resident((1, d), layer),
                  _resident((d, dff), layer), _resident((d, dff), layer), _resident((dff, d), layer)],
        out_specs=pl.BlockSpec((tm, d), lambda i: (i, 0)),
        compiler_params=_cparams(1),
        name="ffn",
    )(x, pre, post, wg, wu, wd)


_MIX_IN = ["x", "pre", "wa", "wrow", "wuv", "wg", "cw", "cb", "clg", "clb", "glg", "glb", "ws", "bsb"]
_Q_SCALE = HEAD_DIM ** -0.5 * LOG2E
_HQ = N_HEADS * HEAD_DIM
_HI = N_IDX_HEADS * IDX_DIM
_HKV = N_KV_HEADS * HEAD_DIM


def _mix_in_body(*refs, names, tm, conv_w, sample, n_seq):
    r = dict(zip(names, refs))
    ca_ref, gm_ref, cs_ref, xin_ref = r["ca"], r["gm"], r["conv_state"], r["xin"]
    cw_ref, cb_ref, clg_ref, clb_ref = r["cw"], r["cb"], r["clg"], r["clb"]
    glg_ref, glb_ref, ws_ref, bsb_ref = r["glg"], r["glb"], r["ws"], r["bsb"]

    h = _rms(r["x"][...], r["pre"][...]).astype(BF)

    def projections():
        row = _dot_nt(h, r["wrow"][...])
        if sample:
            r["q"][...] = (row[:, 0:_HQ] * _Q_SCALE).astype(BF)
            r["qi"][...] = row[:, _HQ:_HQ + _HI].astype(BF)
            o = _HQ + _HI
        else:
            col = _dot_nt(r["wcol"][...], h)
            r["qt"][...] = (col[0:_HQ] * _Q_SCALE).astype(BF)
            r["qit"][...] = col[_HQ:_HQ + _HI].astype(BF)
            vt = col[_HQ + _HI:_HQ + _HI + _HKV].astype(BF)
            tq = r["vt"].shape[2]
            for c in range(tm // tq):
                r["vt"][c] = vt[:, c * tq:(c + 1) * tq]
            r["wit"][...] = col[_HQ + _HI + _HKV:_HQ + _HI + _HKV + N_IDX_HEADS]
            r["ki"][...] = row[:, 2 * _HKV:2 * _HKV + IDX_DIM].astype(BF)
            for g in range(N_KV_HEADS):
                r["kg"][g] = row[:, g * HEAD_DIM:(g + 1) * HEAD_DIM].astype(BF)
            o = 0
        r["kv"][...] = row[:, o:o + 2 * _HKV]
        r["kw"][...] = row[:, o + 2 * _HKV:o + 2 * _HKV + LANES]
        r["gates"][...] = _sigmoid(_dot_nt(h, r["wg"][...])).astype(BF)

    wa_ref, wuv_ref = r["wa"], r["wuv"]

    a = _dot_nt(h, wa_ref[...])
    uv = jax.nn.gelu(_dot_nt(h, wuv_ref[...]))
    cc = a.shape[1] // 2
    glu = a[:, :cc] * _sigmoid(a[:, cc:])
    pre_rows = conv_w - 1
    if sample:
        n_t = tm // n_seq
        xin_ref[0:pre_rows] = r["state"][...]
        for t in range(n_t):
            xin_ref[pre_rows + t] = glu[t * n_seq:(t + 1) * n_seq, :]
        ys = []
        for t in range(n_t):
            y = jnp.zeros((n_seq, cc), F32)
            for j in range(conv_w):
                y = y + xin_ref[t + j] * cw_ref[j:j + 1, :]
            ys.append(y)
        y = jnp.concatenate(ys, axis=0)
        cs_ref[...] = xin_ref[n_t:n_t + pre_rows]
        ca_ref[...] = _silu(_layer_norm(y + cb_ref[...], clg_ref[...], clb_ref[...])).astype(BF)
    else:
        halo = 32
        @pl.when(pl.program_id(1) == 0)
        def _():
            xin_ref[0:halo, :] = jnp.zeros((halo, cc), F32)
        xin_ref[halo:halo + tm, :] = glu
        rb = 64
        off = halo - pre_rows
        sh_ref = r["shifted"]
        for b in range(8):
            n_rows = tm + 8 * (len(range(b, conv_w, 8)) - 1)
            sh_ref[b, 0:n_rows, :] = xin_ref[off + b:off + b + n_rows, :]
        cs_ref[...] = xin_ref[halo + tm - pre_rows:halo + tm, :]
        xin_ref[0:halo, :] = xin_ref[tm:tm + halo, :]

    cg = uv.shape[1] // 2
    u = uv[:, :cg]
    vn = _layer_norm(uv[:, cg:], glg_ref[...], glb_ref[...])
    if sample:
        r["gmlp_v"][...] = vn
    vb = vn.astype(BF)

    projections()

    if not sample:
        for r0 in range(0, tm, rb):
            y = jnp.zeros((rb, cc), F32)
            for j in range(conv_w):
                y = y + sh_ref[j % 8, r0 + j - j % 8:r0 + j - j % 8 + rb, :] * cw_ref[j:j + 1, :]
            y = _silu(_layer_norm(y + cb_ref[...], clg_ref[...], clb_ref[...]))
            ca_ref[r0:r0 + rb, :] = y.astype(BF)

    gd = cg // N_GMLP_GROUPS
    chunk = ws_ref.shape[-1]
    ri = lax.broadcasted_iota(jnp.int32, (chunk, chunk), 0)
    ci = lax.broadcasted_iota(jnp.int32, (chunk, chunk), 1)
    wsm = [jnp.where(ci <= ri, ws_ref[g], 0.0).astype(BF) for g in range(N_GMLP_GROUPS)]
    for n in range(tm // chunk):
        rs = slice(n * chunk, (n + 1) * chunk)
        mixed = jnp.concatenate(
            [_dot(wsm[g], vb[rs, g * gd:(g + 1) * gd]) for g in range(N_GMLP_GROUPS)], axis=1)
        gm_ref[rs, :] = (u[rs, :] * (mixed + bsb_ref[...])).astype(BF)


def _mix_in(x, layer, w, *, tm, n_seq, seq_tiles, sample, state=None):
    n, d = x.shape
    conv_w = w["conv_w"].shape[1]
    cc = w["conv_w"].shape[2]
    cg = w["gmlp_ln_g"].shape[2]
    pre_rows = conv_w - 1
    if sample:
        grid = (1,)
        row = lambda i: (0, 0)
        res = lambda shape: _resident(shape, layer)
        cs_shape = (pre_rows, n_seq, cc)
        cs_spec = pl.BlockSpec(cs_shape, lambda i: (0, 0, 0))
        scratch = {"xin": pltpu.VMEM((pre_rows + tm // n_seq, n_seq, cc), F32)}
    else:
        grid = (n_seq, seq_tiles)
        row = lambda b, t: (b * seq_tiles + t, 0)
        res = lambda shape: _resident(shape, layer)
        cs_shape = (n_seq, pre_rows, cc)
        cs_spec = pl.BlockSpec((None, pre_rows, cc), lambda b, t: (b, 0, 0))
        scratch = {"xin": pltpu.VMEM((32 + tm, cc), F32),
                   "shifted": pltpu.VMEM((8, tm + 8 * ((conv_w - 1) // 8), cc), F32)}

    wrow = w["wrow_s"] if sample else w["wrow_p"]
    in_names = list(_MIX_IN)
    args = [x, w["mix_pre"], w["wa"], wrow, w["wuv"], w["wg"], w["conv_w"], w["conv_b"], w["conv_ln_g"],
            w["conv_ln_b"], w["gmlp_ln_g"], w["gmlp_ln_b"], w["ws"], w["bsb"]]
    in_specs = [pl.BlockSpec((tm, d), row)] + [res(a.shape[1:]) for a in args[1:]]
    if sample:
        in_names.append("state")
        in_specs.append(pl.BlockSpec(cs_shape, lambda i: (0, 0, 0)))
        args.append(state)
    else:
        in_names.append("wcol")
        in_specs.append(res(w["wcol"].shape[1:]))
        args.append(w["wcol"])

    rows = lambda wd, dt: ((n, wd), (tm, wd), row, dt)
    outs = {"kv": rows(2 * _HKV, F32), "kw": rows(LANES, F32), "ca": rows(cc, BF), "gm": rows(cg, BF),
            "gates": rows(w["wg"].shape[1], BF), "conv_state": (cs_shape, cs_spec.block_shape, cs_spec.index_map, F32)}
    if sample:
        outs.update(q=rows(_HQ, BF), qi=rows(_HI, BF), gmlp_v=rows(cg, F32))
    else:
        tq = 256
        cols = lambda ht, dt: ((ht, n), (ht, tm), lambda b, t: (0, b * seq_tiles + t), dt)
        outs.update(qt=cols(_HQ, BF), qit=cols(_HI, BF), wit=cols(N_IDX_HEADS, F32), ki=rows(IDX_DIM, BF),
                    kg=((N_KV_HEADS, n, HEAD_DIM), (N_KV_HEADS, tm, HEAD_DIM),
                        lambda b, t: (0, b * seq_tiles + t, 0), BF),
                    vt=((n // tq, _HKV, tq), (tm // tq, _HKV, tq), lambda b, t: (b * seq_tiles + t, 0, 0), BF))
    out_names = list(outs)

    res_out = pl.pallas_call(
        functools.partial(_mix_in_body, names=in_names + out_names + list(scratch), tm=tm, conv_w=conv_w,
                          sample=sample, n_seq=n_seq),
        out_shape=[jax.ShapeDtypeStruct(outs[k][0], outs[k][3]) for k in out_names],
        grid=grid, in_specs=in_specs,
        out_specs=[pl.BlockSpec(outs[k][1], outs[k][2]) for k in out_names],
        scratch_shapes=list(scratch.values()), compiler_params=_cparams(len(grid)),
        name="mix_in_sample" if sample else "mix_in",
    )(*args)
    return dict(zip(out_names, res_out))


def _ordered_bits_to_float(u):
    t = u ^ jnp.int32(-2 ** 31)
    fb = t ^ (lax.shift_right_arithmetic(t, 31) & jnp.int32(0x7FFFFFFF))
    return lax.bitcast_convert_type(fb, F32)


def _row_total(cnt, ones_bf):
    return _dot(cnt.astype(BF), ones_bf)


def _select_topk(sc_ref, nk, rows, width, k_sel):
    nh = width // LANES
    ones_bf = jnp.ones((LANES, LANES), BF)
    kf = float(k_sel)

    def halves(x):
        return [x[:, i * LANES:(i + 1) * LANES] for i in range(nh)]

    def count(pred):
        def body(c, cnt):
            for xh in halves(sc_ref[c]):
                cnt = cnt + jnp.where(pred(xh), 1.0, 0.0)
            return cnt
        return _row_total(lax.fori_loop(0, nk, body, jnp.zeros((rows, LANES), F32)), ones_bf)

    def bit_body(i, u):
        cand = u | jnp.left_shift(jnp.int32(1), 31 - i)
        thr = _ordered_bits_to_float(cand)
        return jnp.where(count(lambda xh: xh >= thr) >= kf, cand, u)

    u = lax.fori_loop(0, 32, bit_body, jnp.zeros((rows, LANES), jnp.int32))
    thr = _ordered_bits_to_float(u)
    thr = jnp.where(thr >= F32_LOWEST, thr, F32_LOWEST)

    n_ge = count(lambda xh: xh >= thr)

    @pl.when(jnp.max(n_ge) > kf)
    def _():
        need = kf - count(lambda xh: xh > thr)
        ri = lax.broadcasted_iota(jnp.int32, (LANES, LANES), 0)
        ci = lax.broadcasted_iota(jnp.int32, (LANES, LANES), 1)
        before = jnp.where(ri < ci, 1.0, 0.0).astype(BF)

        def fix(c, seen):
            out = []
            for xh in halves(sc_ref[c]):
                tie = xh == thr
                tf = jnp.where(tie, 1.0, 0.0).astype(BF)
                rank = seen + _dot(tf, before)
                out.append(jnp.where(jnp.where(tie, rank, -1.0) >= need, -jnp.inf, xh))
                seen = seen + _dot(tf, ones_bf)
            sc_ref[c] = out[0] if nh == 1 else jnp.concatenate(out, axis=1)
            return seen

        lax.fori_loop(0, nk, fix, jnp.zeros((rows, LANES), F32))

    def to_mask(c, carry):
        x = sc_ref[c]
        thr_w = thr if nh == 1 else jnp.concatenate([thr] * nh, axis=1)
        sc_ref[c] = jnp.where(x >= thr_w, 0.0, MASKED)
        return carry

    lax.fori_loop(0, nk, to_mask, 0)


def _select_topk_cols(sc_ref, hi_ref, lo_ref, nk, n_keys, n_q, k_sel):
    kf = float(k_sel)
    groups = n_keys // 8
    slabs = n_keys // 16
    i16 = jnp.int16
    low_mask = jnp.int32(0xFFFF)
    bias16 = jnp.int32(1 << 15)

    def split(c, carry):
        bits = lax.bitcast_convert_type(sc_ref[c], jnp.int32)
        key = bits ^ (lax.shift_right_arithmetic(bits, 31) & jnp.int32(0x7FFFFFFF))
        hi_ref[c] = lax.shift_right_arithmetic(key, 16).astype(i16)
        lo_ref[c] = ((key & low_mask) - bias16).astype(i16)
        return carry

    lax.fori_loop(0, nk, split, 0)

    def count16(ref, pred):
        def body(c, acc):
            for s in range(slabs):
                acc = acc + jnp.where(pred(ref[c, s * 16:(s + 1) * 16, :]), i16(1), i16(0))
            return acc
        acc = lax.fori_loop(0, nk, body, jnp.zeros((16, n_q), i16))
        return jnp.sum(acc.astype(F32), axis=0, keepdims=True)

    def tile16(v):
        return jnp.broadcast_to(v, (16, n_q)).astype(i16)

    def search16(ref, need):
        def bit_body(i, u):
            cand = u | jnp.left_shift(jnp.int32(1), 15 - i)
            c16 = tile16(cand - bias16)
            return jnp.where(count16(ref, lambda x: x >= c16) >= need, cand, u)
        return lax.fori_loop(0, 16, bit_body, jnp.zeros((1, n_q), jnp.int32)) - bias16

    hi_k = search16(hi_ref, kf)
    hi16 = tile16(hi_k)
    need_lo = kf - count16(hi_ref, lambda x: x > hi16)

    def keep_boundary(c, carry):
        for s in range(slabs):
            rows = slice(s * 16, (s + 1) * 16)
            lo_ref[c, rows, :] = jnp.where(hi_ref[c, rows, :] == hi16, lo_ref[c, rows, :], i16(-(1 << 15)))
        return carry

    lax.fori_loop(0, nk, keep_boundary, 0)
    lo_k = search16(lo_ref, need_lo)
    key_k = jnp.left_shift(hi_k, 16) | (lo_k + bias16)
    thr = lax.bitcast_convert_type(
        key_k ^ (lax.shift_right_arithmetic(key_k, 31) & jnp.int32(0x7FFFFFFF)), F32)
    thr = jnp.where(thr >= F32_LOWEST, thr, F32_LOWEST)

    def count(pred):
        def body(c, cnt):
            x = sc_ref[c].reshape(groups, 8, n_q)
            return cnt + jnp.sum(jnp.where(pred(x), 1.0, 0.0), axis=0)
        cnt = lax.fori_loop(0, nk, body, jnp.zeros((8, n_q), F32))
        return jnp.sum(cnt, axis=0, keepdims=True)

    thr8 = jnp.broadcast_to(thr, (8, n_q))[None]

    @pl.when(jnp.max(count(lambda x: x >= thr8)) > kf)
    def _():
        need = kf - count(lambda x: x > thr8)
        ri = lax.broadcasted_iota(jnp.int32, (n_keys, n_keys), 0)
        ci = lax.broadcasted_iota(jnp.int32, (n_keys, n_keys), 1)
        before = jnp.where(ci < ri, 1.0, 0.0).astype(BF)

        def fix(c, seen):
            x = sc_ref[c]
            tie = x == thr
            tf = jnp.where(tie, 1.0, 0.0)
            rank = seen + _dot(before, tf.astype(BF))
            sc_ref[c] = jnp.where(jnp.where(tie, rank, -1.0) >= need, -jnp.inf, x)
            return seen + jnp.sum(tf, axis=0, keepdims=True)

        lax.fori_loop(0, nk, fix, jnp.zeros((1, n_q), F32))

    def to_mask(c, carry):
        sc_ref[c] = jnp.where(sc_ref[c] >= thr, 0.0, MASKED)
        return carry

    lax.fori_loop(0, nk, to_mask, 0)


def _attn_prompt_body(qt_ref, qit_ref, wit_ref, ki_ref, kg_ref, vt_ref, band_ref, o_ref, sc_ref, hi_ref, lo_ref,
                      *, tq, k_sel, heads_per_stream):
    j = pl.program_id(1)
    nk = j + 1

    def heads_along_lanes(x, heads, width):
        return jnp.concatenate([x[h * width:(h + 1) * width, :] for h in heads], axis=1)

    qi_all = heads_along_lanes(qit_ref[...], range(N_IDX_HEADS), IDX_DIM)
    w_all = heads_along_lanes(wit_ref[...], range(N_IDX_HEADS), 1)
    halves = tq // LANES
    q_pos = lax.broadcasted_iota(jnp.int32, (tq, LANES), 1) + j * tq

    def score_chunk(c, carry):
        k0 = pl.multiple_of(c * tq, tq)
        kic = ki_ref[pl.ds(k0, tq), :]
        k_pos = lax.broadcasted_iota(jnp.int32, (tq, LANES), 0) + k0
        for i in range(halves):
            acc = None
            for h in range(N_IDX_HEADS):
                ls = slice(h * tq + i * LANES, h * tq + (i + 1) * LANES)
                term = jnp.maximum(_dot(kic, qi_all[:, ls]), 0.0) * w_all[:, ls]
                acc = term if acc is None else acc + term
            sc_ref[c, :, i * LANES:(i + 1) * LANES] = jnp.where(k_pos <= q_pos + i * LANES, acc, -jnp.inf)
        return carry

    lax.fori_loop(0, nk, score_chunk, 0)
    _select_topk_cols(sc_ref, hi_ref, lo_ref, nk, tq, tq, k_sel)

    qt = qt_ref[...]
    n_streams = N_HEADS // heads_per_stream
    per_group = GROUP_SIZE // heads_per_stream
    wide = heads_per_stream * tq
    q_s = [heads_along_lanes(qt, range(i * heads_per_stream, (i + 1) * heads_per_stream), HEAD_DIM)
           for i in range(n_streams)]

    def chunk_step(c, carry, bias):
        k0 = pl.multiple_of(c * tq, tq)
        mask = jnp.concatenate([sc_ref[c]] * heads_per_stream, axis=1)
        logits = []
        for i in range(n_streams):
            g = i // per_group
            s = _dot(kg_ref[g, pl.ds(k0, tq), :], q_s[i]) + mask
            if bias is not None:
                s = s + bias(g, slice((i % per_group) * wide, (i % per_group + 1) * wide))
            logits.append(s)
        stats = []
        for i in range(n_streams):
            m, l, _ = carry[i]
            m_new = jnp.maximum(m, jnp.max(logits[i], axis=0, keepdims=True))
            alpha = jnp.exp2(m - m_new)
            p = jnp.exp2(logits[i] - m_new)
            stats.append((m_new, alpha, alpha * l + jnp.sum(p, axis=0, keepdims=True), p.astype(BF)))
        out = []
        for i in range(n_streams):
            g = i // per_group
            m_new, alpha, l_new, p = stats[i]
            vt = vt_ref[c, g * HEAD_DIM:(g + 1) * HEAD_DIM, :]
            out.append((m_new, l_new, alpha * carry[i][2] + _dot(vt, p)))
        return out

    carry = [(jnp.full((1, wide), M_INIT, F32), jnp.zeros((1, wide), F32), jnp.zeros((HEAD_DIM, wide), F32))
             for _ in range(n_streams)]
    j_prev = jnp.maximum(j - 1, 0)
    prev_mask = jnp.where(j > 0, 0.0, MASKED)
    carry = lax.fori_loop(0, j_prev, lambda c, carry: chunk_step(c, carry, None), carry)
    carry = chunk_step(j_prev, carry, lambda g, ls: band_ref[g, 0, :, ls] + prev_mask)
    carry = chunk_step(j, carry, lambda g, ls: band_ref[g, 1, :, ls])
    o_t = []
    for m, l, acc in carry:
        o_s = acc / l
        o_t += [o_s[:, hh * tq:(hh + 1) * tq] for hh in range(heads_per_stream)]
    o_ref[...] = jnp.concatenate(o_t, axis=0).T.astype(BF)


def _attn_prompt(mi, band, *, n_seq, seq, tq, k_sel):
    nt = seq // tq
    col = lambda b, t: (0, b * nt + t)
    hd = N_HEADS * HEAD_DIM
    return pl.pallas_call(
        functools.partial(_attn_prompt_body, tq=tq, k_sel=k_sel, heads_per_stream=2),
        out_shape=jax.ShapeDtypeStruct((n_seq * seq, hd), BF),
        grid=(n_seq, nt),
        in_specs=[pl.BlockSpec((hd, tq), col), pl.BlockSpec((N_IDX_HEADS * IDX_DIM, tq), col),
                  pl.BlockSpec((N_IDX_HEADS, tq), col),
                  pl.BlockSpec((seq, IDX_DIM), lambda b, t: (b, 0)),
                  pl.BlockSpec((N_KV_HEADS, seq, HEAD_DIM), lambda b, t: (0, b, 0)),
                  pl.BlockSpec((nt, N_KV_HEADS * HEAD_DIM, tq), lambda b, t: (b, 0, 0)),
                  _resident(band.shape)],
        out_specs=pl.BlockSpec((tq, hd), lambda b, t: (b * nt + t, 0)),
        scratch_shapes=[pltpu.VMEM((nt, tq, tq), F32), pltpu.VMEM((nt, tq, tq), jnp.int16),
                        pltpu.VMEM((nt, tq, tq), jnp.int16)],
        compiler_params=_cparams(2),
        name="attn_prompt",
    )(mi["qt"], mi["qit"], mi["wit"], mi["ki"], mi["kg"], mi["vt"], band)


def _page_stream(cache_ref, buf_ref, sem_ref, pt_ref, layer, n_pages, page):
    def copy(seq, slot, p):
        return pltpu.make_async_copy(cache_ref.at[layer, pt_ref[seq, p]],
                                     buf_ref.at[slot, :, p * page:(p + 1) * page], sem_ref.at[slot])

    def start(seq, slot):
        for p in range(n_pages):
            copy(seq, slot, p).start(priority=p % 2)

    def wait(seq, slot):
        for p in range(n_pages):
            copy(seq, slot, p).wait()

    return start, wait


def _double_buffered(streams):
    i = pl.program_id(0)
    slot = i % 2

    @pl.when(i == 0)
    def _():
        for start, _ in streams:
            start(0, 0)

    @pl.when(i + 1 < pl.num_programs(0))
    def _():
        for start, _ in streams:
            start(i + 1, 1 - slot)

    for _, wait in streams:
        wait(i, slot)
    return slot


def _idx_scores_body(pt_ref, qi_ref, wq_ref, kin_ref, cki_ref, o_ref, buf_ref, sem_ref,
                     *, layer, n_pages, page, chunk):
    slot = _double_buffered([_page_stream(cki_ref, buf_ref, sem_ref, pt_ref, layer, n_pages, page)])
    qi = qi_ref[...]
    wq = wq_ref[...]
    rows_t = qi.shape[0] // N_IDX_HEADS
    n_new = o_ref.shape[0]
    past = n_pages * page

    def head_sum(s):
        w = s.shape[1]
        wgt = wq if w == LANES else jnp.concatenate([wq] * (w // LANES), axis=1)
        s = (jnp.maximum(s, 0.0) * wgt).reshape(N_IDX_HEADS, rows_t, w)
        acc = s[0]
        for h in range(1, N_IDX_HEADS):
            acc = acc + s[h]
        return acc[0:n_new]

    for c in range(past // chunk):
        cs = slice(c * chunk, (c + 1) * chunk)
        o_ref[:, cs] = head_sum(_dot(qi, buf_ref[slot, :, cs].astype(BF)))
    t = lax.broadcasted_iota(jnp.int32, (n_new, page), 0)
    cc = lax.broadcasted_iota(jnp.int32, (n_new, page), 1)
    o_ref[:, past:past + page] = jnp.where(cc <= t, head_sum(_dot(qi, kin_ref[...])), -jnp.inf)


def _select_body(sc_ref, o_ref, *, k_sel):
    o_ref[...] = sc_ref[...]
    _select_topk(o_ref, 1, o_ref.shape[1], o_ref.shape[2], k_sel)


def _attend_sample_body(pt_ref, q_ref, mask_ref, band_ref, kn_ref, vn_ref, ck_ref, cv_ref, o_ref,
                        kbuf_ref, vbuf_ref, s_ref, ksem_ref, vsem_ref, *, layer, n_pages, page, chunk):
    slot = _double_buffered([_page_stream(ck_ref, kbuf_ref, ksem_ref, pt_ref, layer, n_pages, page),
                             _page_stream(cv_ref, vbuf_ref, vsem_ref, pt_ref, layer, n_pages, page)])
    q = q_ref[...]
    rows = q.shape[0]
    rows_t = rows // N_HEADS
    n_new = mask_ref.shape[0]
    past = n_pages * page
    n_chunks = past // chunk

    def add_mask(s, m):
        w = s.shape[1]
        m = jnp.concatenate([m] + [m[n_new - 1:n_new]] * (rows_t - n_new), axis=0)
        return (s.reshape(N_HEADS, rows_t, w) + m[None]).reshape(rows, w)

    def lane_fold(x, fn, init):
        for i in range(x.shape[1] // LANES):
            init = fn(init, x[:, i * LANES:(i + 1) * LANES])
        return init

    mx = jnp.full((rows, LANES), M_INIT, F32)
    for c in range(n_chunks):
        cs = slice(c * chunk, (c + 1) * chunk)
        s = add_mask(_dot(q, kbuf_ref[slot, :, cs].astype(BF)), mask_ref[:, cs])
        if c == n_chunks - 1:
            s = jnp.concatenate([s[:, :chunk - page], s[:, chunk - page:] + band_ref[:, 0:page]], axis=1)
        s_ref[:, cs] = s
        mx = lane_fold(s, jnp.maximum, mx)
    s_new = add_mask(_dot(q, kn_ref[...]), mask_ref[:, past:past + page]) + band_ref[:, page:2 * page]
    m = jnp.max(jnp.maximum(mx, s_new), axis=1, keepdims=True)

    p_new = jnp.exp2(s_new - m)
    l = p_new
    acc = lax.dot_general(p_new.astype(BF), vn_ref[...], NT_DIMS, preferred_element_type=F32)
    for c in range(n_chunks):
        cs = slice(c * chunk, (c + 1) * chunk)
        p = jnp.exp2(s_ref[:, cs] - m)
        l = lane_fold(p, jnp.add, l)
        acc = acc + lax.dot_general(p.astype(BF), vbuf_ref[slot, :, cs].astype(BF), NT_DIMS,
                                    preferred_element_type=F32)
    o_ref[...] = acc / jnp.sum(l, axis=1, keepdims=True)


def _attn_sample(page_table, qi_rows, wq_rows, q_rows, ki_new_t, k_new_t, v_new_t, band, cki_t, ck_t, cv_t,
                 *, layer, k_sel, n_new):
    n_seq, n_pages = page_table.shape
    page = ck_t.shape[3]
    rows = q_rows.shape[1]
    past = n_pages * page
    width = past + page
    chunk = 1024 if past % 1024 == 0 else page
    per_seq3 = lambda b, pt: (b, 0, 0)
    any_spec = pl.BlockSpec(memory_space=pl.ANY)

    scores = pl.pallas_call(
        functools.partial(_idx_scores_body, layer=layer, n_pages=n_pages, page=page, chunk=chunk),
        out_shape=jax.ShapeDtypeStruct((n_seq, n_new, width), F32),
        grid_spec=pltpu.PrefetchScalarGridSpec(
            num_scalar_prefetch=1, grid=(n_seq,),
            in_specs=[pl.BlockSpec((None, rows, IDX_DIM), per_seq3),
                      pl.BlockSpec((None, rows, LANES), per_seq3),
                      pl.BlockSpec((None, IDX_DIM, page), per_seq3),
                      any_spec],
            out_specs=pl.BlockSpec((None, n_new, width), per_seq3),
            scratch_shapes=[pltpu.VMEM((2, IDX_DIM, past), F32), pltpu.SemaphoreType.DMA((2,))]),
        compiler_params=_cparams(1),
        name="idx_scores_sample",
    )(page_table, qi_rows, wq_rows, ki_new_t, cki_t)

    n_sel = n_seq * n_new
    sel_rows = 64 if n_sel % 64 == 0 else n_sel
    mask = pl.pallas_call(
        functools.partial(_select_body, k_sel=k_sel),
        out_shape=jax.ShapeDtypeStruct((1, n_sel, width), F32),
        grid=(n_sel // sel_rows,),
        in_specs=[pl.BlockSpec((1, sel_rows, width), lambda i: (0, i, 0))],
        out_specs=pl.BlockSpec((1, sel_rows, width), lambda i: (0, i, 0)),
        compiler_params=_cparams(1),
        name="select_sample",
    )(scores.reshape(1, n_sel, width)).reshape(n_seq, n_new, width)

    return pl.pallas_call(
        functools.partial(_attend_sample_body, layer=layer, n_pages=n_pages, page=page, chunk=chunk),
        out_shape=jax.ShapeDtypeStruct((n_seq, rows, LANES), F32),
        grid_spec=pltpu.PrefetchScalarGridSpec(
            num_scalar_prefetch=1, grid=(n_seq,),
            in_specs=[pl.BlockSpec((None, rows, LANES), per_seq3),
                      pl.BlockSpec((None, n_new, width), per_seq3),
                      pl.BlockSpec(band.shape, lambda b, pt: (0, 0), pipeline_mode=pl.Buffered(1)),
                      pl.BlockSpec((None, LANES, page), per_seq3),
                      pl.BlockSpec((None, LANES, page), per_seq3),
                      any_spec, any_spec],
            out_specs=pl.BlockSpec((None, rows, LANES), per_seq3),
            scratch_shapes=[pltpu.VMEM((2, LANES, past), F32), pltpu.VMEM((2, LANES, past), F32),
                            pltpu.VMEM((rows, past), F32),
                            pltpu.SemaphoreType.DMA((2,)), pltpu.SemaphoreType.DMA((2,))]),
        compiler_params=_cparams(1),
        name="attend_sample",
    )(page_table, q_rows, mask, band, k_new_t, v_new_t, ck_t, cv_t)


def _merge_ffn_body(x_ref, ca_ref, att_ref, gm_ref, gt_ref, wco_ref, wao_ref, wgo_ref, wout_ref, post_ref,
                    fpre_ref, fpost_ref, wg_ref, wu_ref, wd_ref, o_ref):
    d = x_ref.shape[1]
    gt = gt_ref[...].astype(F32)
    m = (gt[:, 0:d] * _dot(ca_ref[...], wco_ref[...])
         + gt[:, d:2 * d] * _dot(att_ref[...], wao_ref[...])
         + gt[:, 2 * d:3 * d] * _dot(gm_ref[...], wgo_ref[...]))
    x = x_ref[...] + _rms(_dot(m.astype(BF), wout_ref[...]), post_ref[...])
    o_ref[...] = _ffn_half_step(x, fpre_ref, fpost_ref, wg_ref, wu_ref, wd_ref)


def _merge_ffn(x, ca, att, gm, gates, layer, w, *, tm):
    n, d = x.shape
    row = lambda i: (i, 0)
    weights = [w["wco"], w["wao"], w["wgo"], w["wout"], w["mix_post"], *w["f2"]]
    return pl.pallas_call(
        _merge_ffn_body,
        out_shape=jax.ShapeDtypeStruct((n, d), F32),
        grid=(n // tm,),
        in_specs=[pl.BlockSpec((tm, d), row), pl.BlockSpec((tm, ca.shape[1]), row),
                  pl.BlockSpec((tm, att.shape[1]), row), pl.BlockSpec((tm, gm.shape[1]), row),
                  pl.BlockSpec((tm, gates.shape[1]), row)] + [_resident(a.shape[1:], layer) for a in weights],
        out_specs=pl.BlockSpec((tm, d), row),
        compiler_params=_cparams(1),
        name="merge_ffn",
    )(x, ca, att, gm, gates, *weights)


def _band_body(rb_ref, bucket_ref, o_ref):
    h = pl.program_id(0)
    bucket = bucket_ref[...]
    far = rb_ref[N_BUCKETS - 1, h]
    tile = jnp.zeros(bucket.shape, F32)
    for bkt in range(N_BUCKETS):
        tile = jnp.where(bucket == bkt, rb_ref[bkt, h] - far, tile)
    o_ref[...] = tile * LOG2E


def _band_bias(rel_bias, dist):
    max_exact = N_BUCKETS // 2
    d = jnp.maximum(dist, 1).astype(F32)
    large = max_exact + (jnp.log(d / max_exact) / math.log(MAX_DISTANCE / max_exact)
                         * (N_BUCKETS - max_exact)).astype(jnp.int32)
    bucket = jnp.where(dist < max_exact, dist, jnp.minimum(large, N_BUCKETS - 1))
    r, c = dist.shape
    return pl.pallas_call(
        _band_body,
        out_shape=jax.ShapeDtypeStruct((N_HEADS, r, c), F32),
        grid=(N_HEADS,),
        in_specs=[pl.BlockSpec(memory_space=pltpu.SMEM), pl.BlockSpec((r, c), lambda h: (0, 0))],
        out_specs=pl.BlockSpec((None, r, c), lambda h: (h, 0, 0)),
        compiler_params=_cparams(1),
        name="band_bias",
    )(rel_bias, bucket)


def _prep_weights(p):
    d = p["w_in"].shape[1]
    cc = p["conv_w"].shape[2]
    cg = p["gmlp_ln_g"].shape[1]
    hq = N_HEADS * HEAD_DIM
    hk = N_KV_HEADS * HEAD_DIM
    hi = N_IDX_HEADS * IDX_DIM
    win_t = jnp.swapaxes(p["w_in"], 1, 2).astype(BF)
    o = 0
    cols = {}
    for name, wd in (("a", 2 * cc), ("q", hq), ("k", hk), ("v", hk), ("qi", hi), ("ki", IDX_DIM),
                     ("wi", N_IDX_HEADS), ("uv", 2 * cg), ("g", 3 * d)):
        cols[name] = win_t[:, o:o + wd, :]
        o += wd
    pad = jnp.zeros((win_t.shape[0], LANES - IDX_DIM - N_IDX_HEADS, d), BF)
    vec = lambda a: a[:, None, :]
    kvw = [cols["k"], cols["v"], cols["ki"], cols["wi"], pad]
    return {
        "wa": cols["a"],
        "wrow_p": jnp.concatenate(kvw, axis=1),
        "wrow_s": jnp.concatenate([cols["q"], cols["qi"]] + kvw, axis=1),
        "wcol": jnp.concatenate([cols["q"], cols["qi"], cols["v"], cols["wi"]], axis=1),
        "wuv": cols["uv"], "wg": cols["g"],
        "mix_pre": vec(p["mix_norm_pre"]), "mix_post": vec(p["mix_norm_post"]),
        "conv_w": p["conv_w"], "conv_b": vec(p["conv_b"]),
        "conv_ln_g": vec(p["conv_ln_g"]), "conv_ln_b": vec(p["conv_ln_b"]),
        "gmlp_ln_g": vec(p["gmlp_ln_g"]), "gmlp_ln_b": vec(p["gmlp_ln_b"]),
        "wco": p["w_conv_out"].astype(BF), "wao": p["w_attn_out"].astype(BF),
        "wgo": p["w_gmlp_out"].astype(BF), "wout": p["w_out"].astype(BF),
        "f1": (vec(p["ffn1_norm_pre"]), vec(p["ffn1_norm_post"]), p["ffn1_w_gate"].astype(BF),
               p["ffn1_w_up"].astype(BF), p["ffn1_w_down"].astype(BF)),
        "f2": (vec(p["ffn2_norm_pre"]), vec(p["ffn2_norm_post"]), p["ffn2_w_gate"].astype(BF),
               p["ffn2_w_up"].astype(BF), p["ffn2_w_down"].astype(BF)),
    }


def _gmlp_spatial(ws, bs, rows, n_seq):
    gd = LANES
    if n_seq is None:
        w = ws
        b = jnp.swapaxes(bs, 1, 2)
    else:
        t = rows // n_seq
        eye = jnp.eye(n_seq, dtype=ws.dtype)
        w = jnp.einsum("lgts,bc->lgtbsc", ws[:, :, :t, :t], eye).reshape(ws.shape[0], ws.shape[1], rows, rows)
        b = jnp.repeat(jnp.swapaxes(bs[:, :, :t], 1, 2), n_seq, axis=1)
    return w, jnp.repeat(b, gd, axis=2)


def kernel(x_prompt, x_sample, cache_k, cache_v, cache_idx_k, state_conv, page_table,
           ffn1_norm_pre, ffn1_norm_post, ffn1_w_gate, ffn1_w_up, ffn1_w_down,
           mix_norm_pre, mix_norm_post, w_in, conv_w, conv_b, conv_ln_g, conv_ln_b,
           w_conv_out, w_attn_out, rel_bias, gmlp_ln_g, gmlp_ln_b, gmlp_ws, gmlp_bs,
           w_gmlp_out, w_out, ffn2_norm_pre, ffn2_norm_post, ffn2_w_gate, ffn2_w_up, ffn2_w_down):
    params = dict(ffn1_norm_pre=ffn1_norm_pre, ffn1_norm_post=ffn1_norm_post, ffn1_w_gate=ffn1_w_gate,
                  ffn1_w_up=ffn1_w_up, ffn1_w_down=ffn1_w_down, mix_norm_pre=mix_norm_pre,
                  mix_norm_post=mix_norm_post, w_in=w_in, conv_w=conv_w, conv_b=conv_b,
                  conv_ln_g=conv_ln_g, conv_ln_b=conv_ln_b, w_conv_out=w_conv_out, w_attn_out=w_attn_out,
                  gmlp_ln_g=gmlp_ln_g, gmlp_ln_b=gmlp_ln_b, w_gmlp_out=w_gmlp_out, w_out=w_out,
                  ffn2_norm_pre=ffn2_norm_pre, ffn2_norm_post=ffn2_norm_post, ffn2_w_gate=ffn2_w_gate,
                  ffn2_w_up=ffn2_w_up, ffn2_w_down=ffn2_w_down)
    depth = w_in.shape[0]
    nb, seq, d = x_prompt.shape
    db, t_new, _ = x_sample.shape
    page = cache_k.shape[2]
    n_pages = page_table.shape[1]
    past = n_pages * page
    n_s = db * t_new
    cc = conv_w.shape[2]
    cg = gmlp_ln_g.shape[1]

    w = _prep_weights(params)
    wp = dict(w)
    wp["ws"], wp["bsb"] = _gmlp_spatial(gmlp_ws, gmlp_bs, GMLP_CHUNK, None)
    wsm = dict(w)
    wsm["ws"], wsm["bsb"] = _gmlp_spatial(gmlp_ws, gmlp_bs, n_s, db)

    tq = 256
    tm_p = 512 if seq % 512 == 0 else 256
    k_sel_p = min(TOPK_MAX, seq // 4)
    k_sel_s = min(TOPK_MAX, (past + t_new) // 4)

    r = jnp.arange(tq, dtype=jnp.int32)[None, :]
    c = jnp.arange(2 * tq, dtype=jnp.int32)[:, None]
    band_p = _band_bias(rel_bias, jnp.maximum(tq + r - c, 0))
    band_p = band_p.reshape(N_KV_HEADS, GROUP_SIZE, 2, tq, tq).transpose(0, 2, 3, 1, 4)
    band_p = band_p.reshape(N_KV_HEADS, 2, tq, GROUP_SIZE * tq)
    r8 = jnp.minimum(jnp.arange(8, dtype=jnp.int32), t_new - 1)[:, None]
    c2 = jnp.arange(2 * page, dtype=jnp.int32)[None, :]
    band_s = _band_bias(rel_bias, jnp.maximum(r8 + page - c2, 0)).reshape(N_HEADS * 8, 2 * page)

    ck_t = jnp.transpose(cache_k, (0, 1, 3, 4, 2)).reshape(depth, -1, N_KV_HEADS * HEAD_DIM, page)
    cv_t = jnp.transpose(cache_v, (0, 1, 3, 4, 2)).reshape(depth, -1, N_KV_HEADS * HEAD_DIM, page)
    cki_t = jnp.transpose(cache_idx_k, (0, 1, 3, 2))
    state_t = jnp.swapaxes(state_conv, 1, 2)

    xp = x_prompt.reshape(nb * seq, d)
    xs = jnp.swapaxes(x_sample, 0, 1).reshape(n_s, d)

    def heads_rows(a, width):
        a = a.reshape(t_new, db, -1, width).transpose(1, 2, 0, 3)
        a = jnp.pad(a, ((0, 0), (0, 0), (0, 8 - t_new), (0, 0)), mode="edge")
        return a.reshape(db, -1, width)

    outs = {k: [] for k in ("kp", "vp", "kip", "cp", "ks", "vs", "kis", "cs", "gv")}
    for l in range(depth):
        xp = _ffn(xp, l, *w["f1"], tm=tm_p)
        mi = _mix_in(xp, l, wp, tm=tm_p, n_seq=nb, seq_tiles=seq // tm_p, sample=False)
        att = _attn_prompt(mi, band_p, n_seq=nb, seq=seq, tq=tq, k_sel=k_sel_p)
        xp = _merge_ffn(xp, mi["ca"], att, mi["gm"], mi["gates"], l, w, tm=tm_p)
        kv = mi["kv"].reshape(nb, seq, 2, N_KV_HEADS, HEAD_DIM)
        outs["kp"].append(kv[:, :, 0])
        outs["vp"].append(kv[:, :, 1])
        outs["kip"].append(mi["kw"][:, :IDX_DIM].reshape(nb, seq, IDX_DIM))
        outs["cp"].append(mi["conv_state"])

        xs = _ffn(xs, l, *w["f1"], tm=n_s)
        ms = _mix_in(xs, l, wsm, tm=n_s, n_seq=db, seq_tiles=1, sample=True, state=state_t[l])
        kv_s = jnp.swapaxes(ms["kv"].reshape(t_new, db, 2, N_KV_HEADS * HEAD_DIM), 0, 1)
        ki_s = jnp.swapaxes(ms["kw"][:, :IDX_DIM].reshape(t_new, db, IDX_DIM), 0, 1)
        qi_rows = heads_rows(ms["qi"], IDX_DIM)
        wq_rows = heads_rows(jnp.broadcast_to(
            ms["kw"][:, IDX_DIM:IDX_DIM + N_IDX_HEADS, None], (n_s, N_IDX_HEADS, LANES)
        ).reshape(n_s, N_IDX_HEADS * LANES), LANES)
        qh = heads_rows(ms["q"], HEAD_DIM)
        zero = jnp.zeros_like(qh)
        first = (jnp.arange(N_HEADS * 8) < GROUP_SIZE * 8)[None, :, None]
        q_rows = jnp.concatenate([jnp.where(first, qh, zero), jnp.where(first, zero, qh)], axis=-1)
        new_t = lambda a: jnp.pad(jnp.swapaxes(a, 1, 2), ((0, 0), (0, 0), (0, page - t_new))).astype(BF)
        o = _attn_sample(page_table, qi_rows, wq_rows, q_rows, new_t(ki_s), new_t(kv_s[:, :, 0]),
                         new_t(kv_s[:, :, 1]), band_s, cki_t, ck_t, cv_t, layer=l, k_sel=k_sel_s, n_new=t_new)
        o = o.reshape(db, N_HEADS, 8, N_KV_HEADS, HEAD_DIM)[:, :, :t_new]
        o = jnp.concatenate([o[:, :GROUP_SIZE, :, 0], o[:, GROUP_SIZE:, :, 1]], axis=1)
        att_s = o.transpose(2, 0, 1, 3).reshape(n_s, N_HEADS * HEAD_DIM).astype(BF)
        xs = _merge_ffn(xs, ms["ca"], att_s, ms["gm"], ms["gates"], l, w, tm=n_s)
        outs["ks"].append(kv_s[:, :, 0].reshape(db, t_new, N_KV_HEADS, HEAD_DIM))
        outs["vs"].append(kv_s[:, :, 1].reshape(db, t_new, N_KV_HEADS, HEAD_DIM))
        outs["kis"].append(ki_s)
        outs["cs"].append(jnp.swapaxes(ms["conv_state"], 0, 1))
        outs["gv"].append(jnp.swapaxes(ms["gmlp_v"].reshape(t_new, db, cg), 0, 1))

    yp = xp.reshape(nb, seq, d)
    ys = jnp.swapaxes(xs.reshape(t_new, db, d), 0, 1)
    st = lambda k: jnp.stack(outs[k])
    return (yp, ys, st("kp"), st("vp"), st("kip"), st("cp"),
            st("ks"), st("vs"), st("kis"), st("cs"), st("gv"))
```
